```python
import math
import jax, jax.numpy as jnp
from jax import lax
import numpy as np

D_MODEL = 1024
BATCH = 4
SEQ = 8192
DEPTH = 1
DEC_BATCH = 128
DEC_SEQ = 1
PAST_LEN = 8192
PAGE_SIZE = 128

M_HEADS = 4
M_DK = 128
M_DV = 128
M_WIDTH = M_HEADS * M_DV
M_CHUNK = 64
A_HEADS = 4
A_DH = 128
A_WIDTH = A_HEADS * A_DH
ROT_DIM = A_DH // 4
ROPE_THETA = 500000.0
IDX_HEADS = 8
IDX_DIM = 64
IDX_ROT = IDX_DIM // 4
TOPK_MAX = 256
Q_BLOCK = 128
MIX_WIDTH = M_WIDTH + A_WIDTH
DEEPNORM_ALPHA = (2.0 * DEPTH) ** 0.25
DEEPNORM_BETA = (8.0 * DEPTH) ** -0.25
LN_EPS = 1e-5

IN_SPLITS = (
    ("m_q", M_HEADS * M_DK), ("m_k", M_HEADS * M_DK), ("m_v", M_WIDTH),
    ("m_o", M_WIDTH), ("m_z", M_WIDTH), ("m_i", M_HEADS), ("m_f", M_HEADS),
    ("a_q", A_WIDTH), ("a_k", A_WIDTH), ("a_v", A_WIDTH), ("a_z", A_WIDTH),
    ("i_q", IDX_HEADS * IDX_DIM), ("i_k", IDX_DIM), ("i_w", IDX_HEADS),
)
IN_WIDTH = sum(n for _, n in IN_SPLITS)

kernel_name = "hymba_mlstm_dsa_step"


def _split_offsets():
    offs, start = {}, 0
    for name, n in IN_SPLITS:
        offs[name] = (start, start + n)
        start += n
    return offs


def _rope(x, pos, rot_dim):
    half = rot_dim // 2
    inv = ROPE_THETA ** (-jnp.arange(half, dtype=jnp.float32) / half)
    ang = pos.astype(jnp.float32)[:, None] * inv
    cos = jnp.cos(ang)[:, None, :]
    sin = jnp.sin(ang)[:, None, :]
    xr = x[..., :rot_dim].astype(jnp.float32)
    x1, x2 = xr[..., :half], xr[..., half:]
    r = jnp.concatenate([x1 * cos - x2 * sin, x2 * cos + x1 * sin], axis=-1)
    return jnp.concatenate([r.astype(x.dtype), x[..., rot_dim:]], axis=-1)


def _layernorm(x, g, b):
    xf = x.astype(jnp.float32)
    mu = jnp.mean(xf, axis=-1, keepdims=True)
    var = jnp.mean(jnp.square(xf - mu), axis=-1, keepdims=True)
    return ((xf - mu) * lax.rsqrt(var + LN_EPS) * g + b).astype(x.dtype)


def _head_norm(h, g):
    mu = jnp.mean(h, axis=-1, keepdims=True)
    var = jnp.mean(jnp.square(h - mu), axis=-1, keepdims=True)
    hn = (h - mu) * lax.rsqrt(var + LN_EPS)
    return hn.reshape(h.shape[:2] + (-1,)) * g


def _branch_inputs(x, pos, w_in, b_in):
    B, S, _ = x.shape
    p = jnp.einsum("bsd,de->bse", x, w_in) + b_in
    offs = _split_offsets()
    part = {nm: p[..., a:b] for nm, (a, b) in offs.items()}
    heads = lambda t, h: t.reshape(B, S, h, -1)
    return dict(
        m_q=heads(part["m_q"], M_HEADS),
        m_k=heads(part["m_k"], M_HEADS) * (M_DK ** -0.5),
        m_v=heads(part["m_v"], M_HEADS),
        m_o=part["m_o"], m_z=part["m_z"],
        m_i=part["m_i"], m_f=part["m_f"],
        a_q=_rope(heads(part["a_q"], A_HEADS), pos, ROT_DIM),
        a_k=_rope(heads(part["a_k"], A_HEADS), pos, ROT_DIM),
        a_v=heads(part["a_v"], A_HEADS),
        a_z=part["a_z"],
        i_q=_rope(heads(part["i_q"], IDX_HEADS), pos, IDX_ROT),
        i_k=_rope(part["i_k"][:, :, None, :], pos, IDX_ROT)[:, :, 0, :],
        i_w=part["i_w"],
    )


def _mlstm_chunkwise(q, k, v, i_pre, f_pre):
    B, S, H, DK = q.shape
    DV = v.shape[-1]
    nc = S // M_CHUNK
    f32 = jnp.float32

    def to_chunks(t):
        t = t.astype(f32).reshape((B, nc, M_CHUNK) + t.shape[2:])
        return jnp.moveaxis(jnp.moveaxis(t, 3, 2), 1, 0)

    qc, kc, vc = to_chunks(q), to_chunks(k), to_chunks(v)
    lic = to_chunks(i_pre)
    lfc = to_chunks(jax.nn.log_sigmoid(f_pre.astype(f32)))
    causal = jnp.tril(jnp.ones((M_CHUNK, M_CHUNK), dtype=bool))

    def step(carry, inp):
        C, n, m = carry
        qb, kb, vb, li, lf = inp
        bcum = jnp.cumsum(lf, axis=-1)
        dmat = bcum[..., :, None] - bcum[..., None, :] + li[..., None, :]
        dmat = jnp.where(causal, dmat, -jnp.inf)
        inter = bcum + m[..., None]
        m_row = jnp.maximum(inter, jnp.max(dmat, axis=-1))
        w_intra = jnp.exp(dmat - m_row[..., None])
        w_inter = jnp.exp(inter - m_row)
        qk = jnp.einsum("bhtk,bhsk->bhts", qb, kb) * w_intra
        num = (w_inter[..., None] * jnp.einsum("bhtk,bhkv->bhtv", qb, C)
               + jnp.einsum("bhts,bhsv->bhtv", qk, vb))
        den = w_inter * jnp.einsum("bhtk,bhk->bht", qb, n) + jnp.sum(qk, axis=-1)
        h = num / jnp.maximum(jnp.abs(den), jnp.exp(-m_row))[..., None]
        b_last = bcum[..., -1]
        g = b_last[..., None] - bcum + li
        m_new = jnp.maximum(b_last + m, jnp.max(g, axis=-1))
        decay = jnp.exp(b_last + m - m_new)
        wk = jnp.exp(g - m_new[..., None])
        C_new = decay[..., None, None] * C + jnp.einsum("bhs,bhsk,bhsv->bhkv", wk, kb, vb)
        n_new = decay[..., None] * n + jnp.einsum("bhs,bhsk->bhk", wk, kb)
        return (C_new, n_new, m_new), h

    init = (jnp.zeros((B, H, DK, DV), f32), jnp.zeros((B, H, DK), f32), jnp.zeros((B, H), f32))
    (C, n, m), hc = lax.scan(step, init, (qc, kc, vc, lic, lfc))
    h = jnp.moveaxis(jnp.moveaxis(hc, 0, 1), 2, 3).reshape(B, S, H, DV)
    return h, C, n, m


def _mlstm_recurrent(q, k, v, i_pre, f_pre, C0, n0, m0):
    f32 = jnp.float32
    xs = tuple(jnp.moveaxis(t.astype(f32), 1, 0) for t in (q, k, v, i_pre, f_pre))

    def step(carry, inp):
        C, n, m = carry
        qt, kt, vt, it, ft = inp
        lf = jax.nn.log_sigmoid(ft)
        m_new = jnp.maximum(lf + m, it)
        fg = jnp.exp(lf + m - m_new)
        ig = jnp.exp(it - m_new)
        C = fg[..., None, None] * C + ig[..., None, None] * kt[..., :, None] * vt[..., None, :]
        n = fg[..., None] * n + ig[..., None] * kt
        num = jnp.einsum("bhk,bhkv->bhv", qt, C)
        den = jnp.einsum("bhk,bhk->bh", qt, n)
        h = num / jnp.maximum(jnp.abs(den), jnp.exp(-m_new))[..., None]
        return (C, n, m_new), h

    init = (C0.astype(f32), n0.astype(f32), m0.astype(f32))
    (C, n, m), hs = lax.scan(step, init, xs)
    return jnp.moveaxis(hs, 0, 1), C, n, m


def _indexer_scores(q_idx, w_idx, k_idx):
    dots = jnp.einsum("bthd,bld->bthl", q_idx, k_idx).astype(jnp.float32) * (IDX_DIM ** -0.5)
    w = w_idx.astype(jnp.float32) * (IDX_HEADS ** -0.5)
    return jnp.einsum("bth,bthl->btl", w, jax.nn.relu(dots))


def _sparse_attend(q, kg, vg, valid):
    s = jnp.einsum("bthd,btkhd->bthk", q, kg).astype(jnp.float32) * (A_DH ** -0.5)
    s = jnp.where(valid[:, :, None, :], s, -jnp.inf)
    p = jax.nn.softmax(s, axis=-1)
    return jnp.einsum("bthk,btkhd->bthd", p.astype(vg.dtype), vg)


def _gather_rows(t, idx):
    return jax.vmap(lambda tb, ib: tb[ib])(t, idx)


def _dsa_prompt(q, k, v, q_idx, k_idx, w_idx):
    B, S, H, D = q.shape
    topk = min(TOPK_MAX, S // 4)
    key_pos = jnp.arange(S)

    def block(i):
        start = i * Q_BLOCK
        qb = lax.dynamic_slice_in_dim(q, start, Q_BLOCK, axis=1)
        qib = lax.dynamic_slice_in_dim(q_idx, start, Q_BLOCK, axis=1)
        wb = lax.dynamic_slice_in_dim(w_idx, start, Q_BLOCK, axis=1)
        q_pos = start + jnp.arange(Q_BLOCK)
        sc = _indexer_scores(qib, wb, k_idx)
        sc = jnp.where(key_pos[None, :] <= q_pos[:, None], sc, -jnp.inf)
        _, idx = lax.top_k(sc, topk)
        valid = idx <= q_pos[None, :, None]
        return _sparse_attend(qb, _gather_rows(k, idx), _gather_rows(v, idx), valid)

    out = lax.map(block, jnp.arange(S // Q_BLOCK))
    return jnp.moveaxis(out, 0, 1).reshape(B, S, H, D)


def _dsa_sample(q, k_new, v_new, q_idx, kidx_new, w_idx, cache_k, cache_v, cache_kidx, page_table):
    DB, T = q.shape[:2]
    n_pages = page_table.shape[1]
    past = n_pages * PAGE_SIZE
    L = past + T
    topk = min(TOPK_MAX, L // 4)
    kidx_past = cache_kidx[page_table].reshape(DB, past, IDX_DIM)
    kidx_all = jnp.concatenate([kidx_past.astype(kidx_new.dtype), kidx_new], axis=1)
    sc = _indexer_scores(q_idx, w_idx, kidx_all)
    q_pos = past + jnp.arange(T)
    sc = jnp.where(jnp.arange(L)[None, :] <= q_pos[:, None], sc, -jnp.inf)
    _, idx = lax.top_k(sc, topk)
    valid = idx <= q_pos[None, :, None]
    in_past = idx < past
    pidx = jnp.minimum(idx, past - 1)
    phys = page_table[jnp.arange(DB)[:, None, None], pidx // PAGE_SIZE]
    off = pidx % PAGE_SIZE
    nidx = jnp.clip(idx - past, 0, T - 1)
    sel = in_past[..., None, None]
    kg = jnp.where(sel, cache_k[phys, off].astype(k_new.dtype), _gather_rows(k_new, nidx))
    vg = jnp.where(sel, cache_v[phys, off].astype(v_new.dtype), _gather_rows(v_new, nidx))
    return _sparse_attend(q, kg, vg, valid)


def _branch_outputs(h_m, m_o, m_z, h_a, a_z, mh_gain, w_out):
    B, S = h_m.shape[:2]
    f32 = jnp.float32
    h_m = h_m * jax.nn.sigmoid(m_o.astype(f32)).reshape(B, S, M_HEADS, M_DV)
    g_m = _head_norm(h_m, mh_gain) * jax.nn.silu(m_z.astype(f32))
    g_a = h_a.reshape(B, S, A_WIDTH).astype(f32) * jax.nn.silu(a_z.astype(f32))
    mixed = jnp.concatenate([g_m, g_a], axis=-1).astype(w_out.dtype)
    return jnp.einsum("bse,ed->bsd", mixed, w_out)


def setup_inputs(seed: int = 0) -> dict:
    key = jax.random.key(seed)
    ks = jax.random.split(key, 16)
    nrm = jax.random.normal
    n_pages = PAST_LEN // PAGE_SIZE
    n_used = DEC_BATCH * n_pages
    n_pool = n_used + max(1, n_used // 4)
    x_prompt = nrm(ks[0], (BATCH, SEQ, D_MODEL), jnp.float32)
    x_sample = nrm(ks[1], (DEC_BATCH, DEC_SEQ, D_MODEL), jnp.float32)
    state_C = 0.1 * nrm(ks[2], (DEPTH, DEC_BATCH, M_HEADS, M_DK, M_DV), jnp.float32)
    state_n = 0.1 * nrm(ks[3], (DEPTH, DEC_BATCH, M_HEADS, M_DK), jnp.float32)
    state_m = nrm(ks[4], (DEPTH, DEC_BATCH, M_HEADS), jnp.float32)
    cache_k = nrm(ks[5], (DEPTH, n_pool, PAGE_SIZE, A_HEADS, A_DH), jnp.float32)
    cache_v = nrm(ks[6], (DEPTH, n_pool, PAGE_SIZE, A_HEADS, A_DH), jnp.float32)
    cache_kidx = nrm(ks[7], (DEPTH, n_pool, PAGE_SIZE, IDX_DIM), jnp.float32)
    page_table = jax.random.permutation(ks[8], n_pool)[:n_used].reshape(DEC_BATCH, n_pages).astype(jnp.int32)
    w_in = nrm(ks[9], (DEPTH, D_MODEL, IN_WIDTH), jnp.float32) * (D_MODEL ** -0.5)
    f_a, f_b = _split_offsets()["m_f"]
    b_in = 0.02 * nrm(ks[10], (DEPTH, IN_WIDTH), jnp.float32)
    b_in = b_in.at[:, f_a:f_b].add(jnp.linspace(3.0, 6.0, M_HEADS, dtype=jnp.float32))
    mh_gain = 1.0 + 0.02 * nrm(ks[11], (DEPTH, M_WIDTH), jnp.float32)
    w_out = nrm(ks[12], (DEPTH, MIX_WIDTH, D_MODEL), jnp.float32) * (MIX_WIDTH ** -0.5) * DEEPNORM_BETA
    ln_g = 1.0 + 0.02 * nrm(ks[13], (DEPTH, D_MODEL), jnp.float32)
    ln_b = 0.02 * nrm(ks[14], (DEPTH, D_MODEL), jnp.float32)
    return {"x_prompt": x_prompt, "x_sample": x_sample,
            "state_C": state_C, "state_n": state_n, "state_m": state_m,
            "cache_k": cache_k, "cache_v": cache_v, "cache_kidx": cache_kidx,
            "page_table": page_table,
            "w_in": w_in, "b_in": b_in, "mh_gain": mh_gain, "w_out": w_out,
            "ln_g": ln_g, "ln_b": ln_b}


def reference(x_prompt, x_sample, state_C, state_n, state_m, cache_k, cache_v, cache_kidx,
              page_table, w_in, b_in, mh_gain, w_out, ln_g, ln_b):
    pos_p = jnp.arange(x_prompt.shape[1])
    pos_s = PAST_LEN + jnp.arange(x_sample.shape[1])
    hp, hs = x_prompt, x_sample
    kp, vp, ip, Cp, np_, mp = [], [], [], [], [], []
    ks_, vs_, is_, Cs, ns_, ms_ = [], [], [], [], [], []
    for l in range(DEPTH):
        br = _branch_inputs(hp, pos_p, w_in[l], b_in[l])
        h_m, C1, n1, m1 = _mlstm_chunkwise(br["m_q"], br["m_k"], br["m_v"], br["m_i"], br["m_f"])
        h_a = _dsa_prompt(br["a_q"], br["a_k"], br["a_v"], br["i_q"], br["i_k"], br["i_w"])
        sub = _branch_outputs(h_m, br["m_o"], br["m_z"], h_a, br["a_z"], mh_gain[l], w_out[l])
        hp_new = _layernorm(DEEPNORM_ALPHA * hp + sub, ln_g[l], ln_b[l])
        kp.append(br["a_k"]); vp.append(br["a_v"]); ip.append(br["i_k"])
        Cp.append(C1.astype(hp.dtype)); np_.append(n1.astype(hp.dtype)); mp.append(m1.astype(hp.dtype))
        hp = hp_new
        bs = _branch_inputs(hs, pos_s, w_in[l], b_in[l])
        g_m, C2, n2, m2 = _mlstm_recurrent(bs["m_q"], bs["m_k"], bs["m_v"], bs["m_i"], bs["m_f"],
                                           state_C[l], state_n[l], state_m[l])
        g_a = _dsa_sample(bs["a_q"], bs["a_k"], bs["a_v"], bs["i_q"], bs["i_k"], bs["i_w"],
                          cache_k[l], cache_v[l], cache_kidx[l], page_table)
        sub_s = _branch_outputs(g_m, bs["m_o"], bs["m_z"], g_a, bs["a_z"], mh_gain[l], w_out[l])
        hs_new = _layernorm(DEEPNORM_ALPHA * hs + sub_s, ln_g[l], ln_b[l])
        ks_.append(bs["a_k"]); vs_.append(bs["a_v"]); is_.append(bs["i_k"])
        Cs.append(C2.astype(state_C.dtype)); ns_.append(n2.astype(state_n.dtype)); ms_.append(m2.astype(state_m.dtype))
        hs = hs_new
    return (hp, hs,
            jnp.stack(kp), jnp.stack(vp), jnp.stack(ip), jnp.stack(Cp), jnp.stack(np_), jnp.stack(mp),
            jnp.stack(ks_), jnp.stack(vs_), jnp.stack(is_), jnp.stack(Cs), jnp.stack(ns_), jnp.stack(ms_))
```

```python
import functools

import jax
import jax.numpy as jnp
from jax import lax
from jax.experimental import pallas as pl
from jax.experimental.pallas import tpu as pltpu

F32 = jnp.float32
BF16 = jnp.bfloat16
I32 = jnp.int32

D_MODEL = 1024
PAGE_SIZE = 128
M_HEADS = 4
M_DK = 128
M_DV = 128
M_WIDTH = M_HEADS * M_DV
A_HEADS = 4
A_DH = 128
A_WIDTH = A_HEADS * A_DH
ROT_DIM = A_DH // 4
ROPE_THETA = 500000.0
IDX_HEADS = 8
IDX_DIM = 64
IDX_ROT = IDX_DIM // 4
TOPK_MAX = 256
MIX_WIDTH = M_WIDTH + A_WIDTH
LN_EPS = 1e-5

LANES = 128
NEG_BIG = -1e30
INT_MIN = -(2 ** 31)

_SPLITS = (
    ("m_q", M_WIDTH), ("m_k", M_WIDTH), ("m_v", M_WIDTH), ("m_o", M_WIDTH), ("m_z", M_WIDTH),
    ("m_i", M_HEADS), ("m_f", M_HEADS),
    ("a_q", A_WIDTH), ("a_k", A_WIDTH), ("a_v", A_WIDTH), ("a_z", A_WIDTH),
    ("i_q", IDX_HEADS * IDX_DIM), ("i_k", IDX_DIM), ("i_w", IDX_HEADS),
)
_BIG = ("m_q", "m_k", "m_v", "m_o", "m_z", "a_q", "a_k", "a_v", "a_z", "i_q")
_SM_I = IDX_DIM
_SM_F = IDX_DIM + M_HEADS
_SM_W = IDX_DIM + 2 * M_HEADS
VMEM_LIMIT = 56 * 1024 * 1024


def _offsets():
    offs, start = {}, 0
    for name, n in _SPLITS:
        offs[name] = (start, start + n)
        start += n
    return offs


def _log_sigmoid(x):
    return jnp.minimum(x, 0.0) - jnp.log1p(jnp.exp(-jnp.abs(x)))


def _sigmoid(x):
    return 1.0 / (1.0 + jnp.exp(-x))


def _rope_tables(pos, rot_dim, period):
    half = rot_dim // 2
    inv = ROPE_THETA ** (-jnp.arange(half, dtype=F32) / half)
    ang = pos.astype(F32)[:, None] * inv
    cos, sin = jnp.cos(ang), jnp.sin(ang)
    n = pos.shape[0]
    c = jnp.concatenate([cos, cos, jnp.ones((n, period - rot_dim), F32)], axis=-1)
    s = jnp.concatenate([sin, sin, jnp.zeros((n, period - rot_dim), F32)], axis=-1)
    reps = LANES // period
    return jnp.tile(c, (1, reps)), jnp.tile(s, (1, reps))


def _rope128(x, c, s, half, period):
    lane = lax.broadcasted_iota(I32, x.shape, 1) % period
    up = pltpu.roll(x, LANES - half, axis=1)
    dn = pltpu.roll(x, half, axis=1)
    t = jnp.where(lane < half, -up, dn)
    return x * c + t * s


def _proj_kernel(x_ref, wb_ref, ws_ref, wg_ref, bb_ref, bs_ref, bg_ref,
                 ch_ref, sh_ref, ci_ref, si_ref,
                 mq_ref, mk_ref, mv_ref, mo_ref, mz_ref, aq_ref, ak_ref, av_ref, az_ref,
                 iq_ref, ik_ref, sm_ref, gt_ref, akb_ref, avb_ref, ikb_ref):
    x = x_ref[...].astype(BF16)

    def piece(j):
        w = wb_ref[:, j * 512:(j + 1) * 512]
        return jnp.dot(x, w, preferred_element_type=F32) + bb_ref[:, j * 512:(j + 1) * 512]

    def rope512(p, c, s, half, period):
        return jnp.concatenate(
            [_rope128(p[:, g * LANES:(g + 1) * LANES], c, s, half, period) for g in range(4)], axis=-1)

    ch, sh = ch_ref[...], sh_ref[...]
    ci, si = ci_ref[...], si_ref[...]
    mq_ref[...] = piece(0).astype(mq_ref.dtype)
    mk_ref[...] = (piece(1) * (M_DK ** -0.5)).astype(mk_ref.dtype)
    mv_ref[...] = piece(2).astype(mv_ref.dtype)
    mo_ref[...] = piece(3)
    mz_ref[...] = piece(4)
    aq = rope512(piece(5), ch, sh, ROT_DIM // 2, LANES)
    aq_ref[...] = (aq * (A_DH ** -0.5)).astype(BF16)
    ak = rope512(piece(6), ch, sh, ROT_DIM // 2, LANES)
    ak_ref[...] = ak
    akb_ref[...] = ak.astype(BF16)
    av = piece(7)
    av_ref[...] = av
    avb_ref[...] = av.astype(BF16)
    az_ref[...] = piece(8)
    iq_ref[...] = rope512(piece(9), ci, si, IDX_ROT // 2, IDX_DIM).astype(BF16)
    psm = jnp.dot(x, ws_ref[...], preferred_element_type=F32) + bs_ref[...]
    sm_ref[...] = psm
    ik = _rope128(psm, ci, si, IDX_ROT // 2, IDX_DIM)[:, :IDX_DIM]
    ik_ref[...] = ik
    ikb_ref[...] = ik.astype(BF16)
    gt = lax.dot_general(wg_ref[...], x, (((1,), (1,)), ((), ())), preferred_element_type=F32)
    gt_ref[...] = gt + bg_ref[...]


def _const_spec(shape):
    nd = len(shape)
    return pl.BlockSpec(shape, lambda *_: (0,) * nd, pipeline_mode=pl.Buffered(1))


def _project(x2d, pos, weights, n_batch, tm, qkv_dtype):
    wb, ws, wg, bb, bs, bg = weights
    n_rows = x2d.shape[0]
    n_pos = pos.shape[0]
    npb = n_pos // tm
    ch, sh = _rope_tables(pos, ROT_DIM, LANES)
    ci, si = _rope_tables(pos, IDX_ROT, IDX_DIM)

    row = lambda p, b: (b * npb + p, 0)
    tab = lambda p, b: (p, 0)
    wide = lambda dt: jax.ShapeDtypeStruct((n_rows, 512), dt)
    out_shape = (
        wide(qkv_dtype), wide(qkv_dtype), wide(qkv_dtype), wide(F32), wide(F32),
        wide(BF16), wide(F32), wide(F32), wide(F32),
        wide(BF16),
        jax.ShapeDtypeStruct((n_rows, IDX_DIM), F32),
        jax.ShapeDtypeStruct((n_rows, LANES), F32),
        jax.ShapeDtypeStruct((8, n_rows), F32),
        wide(BF16), wide(BF16),
        jax.ShapeDtypeStruct((n_rows, IDX_DIM), BF16),
    )
    out_specs = (
        [pl.BlockSpec((tm, 512), row)] * 10
        + [pl.BlockSpec((tm, IDX_DIM), row), pl.BlockSpec((tm, LANES), row),
           pl.BlockSpec((8, tm), lambda p, b: (0, b * npb + p)),
           pl.BlockSpec((tm, 512), row), pl.BlockSpec((tm, 512), row),
           pl.BlockSpec((tm, IDX_DIM), row)]
    )
    in_specs = [
        pl.BlockSpec((tm, D_MODEL), row),
        _const_spec(wb.shape), _const_spec(ws.shape), _const_spec(wg.shape),
        _const_spec(bb.shape), _const_spec(bs.shape), _const_spec(bg.shape),
        pl.BlockSpec((tm, LANES), tab), pl.BlockSpec((tm, LANES), tab),
        pl.BlockSpec((tm, LANES), tab), pl.BlockSpec((tm, LANES), tab),
    ]
    return pl.pallas_call(
        _proj_kernel,
        grid=(npb, n_batch),
        in_specs=in_specs,
        out_specs=out_specs,
        out_shape=out_shape,
        compiler_params=pltpu.CompilerParams(
            dimension_semantics=("arbitrary", "arbitrary"), vmem_limit_bytes=VMEM_LIMIT),
        name="proj_rope",
    )(x2d, wb, ws, wg, bb, bs, bg, ch, sh, ci, si)


def _prep_weights(w_in, b_in):
    offs = _offsets()
    col = lambda nm: w_in[:, offs[nm][0]:offs[nm][1]]
    bcol = lambda nm: b_in[offs[nm][0]:offs[nm][1]]
    wb = jnp.concatenate([col(nm) for nm in _BIG], axis=1).astype(BF16)
    bb = jnp.concatenate([bcol(nm) for nm in _BIG])[None, :]
    pad = LANES - (IDX_DIM + 2 * M_HEADS + IDX_HEADS)
    ws = jnp.concatenate([col("i_k"), col("m_i"), col("m_f"), col("i_w"),
                          jnp.zeros((D_MODEL, pad), w_in.dtype)], axis=1).astype(BF16)
    bs = jnp.concatenate([bcol("i_k"), bcol("m_i"), bcol("m_f"), bcol("i_w"),
                          jnp.zeros((pad,), b_in.dtype)])[None, :]
    wg = jnp.concatenate([col("m_i"), col("m_f")], axis=1).T.astype(BF16)
    bg = jnp.concatenate([bcol("m_i"), bcol("m_f")])[:, None]
    return wb, ws, wg, bb, bs, bg


def _cumsum_rows(x):
    n = x.shape[0]
    row = lax.broadcasted_iota(I32, x.shape, 0)
    s = 1
    while s < n:
        x = x + jnp.where(row >= s, pltpu.roll(x, s, axis=0), 0.0)
        s *= 2
    return x


def _cumsum_lanes(x):
    n = x.shape[1]
    lane = lax.broadcasted_iota(I32, x.shape, 1)
    s = 1
    while s < n:
        x = x + jnp.where(lane >= s, pltpu.roll(x, s, axis=1), 0.0)
        s *= 2
    return x


def _mlstm_kernel(mq_ref, mk_ref, mv_ref, sm_ref, gt_ref, h_ref, c_ref, n_ref, m_ref):
    c_idx = pl.program_id(1)
    L = mq_ref.shape[0]

    @pl.when(c_idx == 0)
    def _():
        c_ref[...] = jnp.zeros_like(c_ref)
        n_ref[...] = jnp.zeros_like(n_ref)
        m_ref[...] = jnp.zeros_like(m_ref)

    sm = sm_ref[...]
    gt = gt_ref[...]
    bcol = _cumsum_rows(_log_sigmoid(sm))
    brow = _cumsum_lanes(_log_sigmoid(gt))
    tpos = lax.broadcasted_iota(I32, (L, L), 0)
    spos = lax.broadcasted_iota(I32, (L, L), 1)
    causal = tpos >= spos

    for h in range(M_HEADS):
        hs = slice(h * M_DK, (h + 1) * M_DK)
        q = mq_ref[:, hs]
        k = mk_ref[:, hs]
        v = mv_ref[:, hs]
        li_c = sm[:, _SM_I + h:_SM_I + h + 1]
        b_c = bcol[:, _SM_F + h:_SM_F + h + 1]
        li_r = gt[h:h + 1, :]
        b_r = brow[M_HEADS + h:M_HEADS + h + 1, :]
        m_prev = m_ref[0, h][:, :1]
        c_prev = c_ref[0, h]
        n_prev = n_ref[0, h]

        dmat = jnp.where(causal, b_c + (li_r - b_r), -jnp.inf)
        inter = b_c + m_prev
        m_row = jnp.maximum(inter, jnp.max(dmat, axis=-1, keepdims=True))
        w_intra = jnp.exp(dmat - m_row)
        w_inter = jnp.exp(inter - m_row)
        s = lax.dot_general(q, k, (((1,), (1,)), ((), ())), preferred_element_type=F32)
        qk = s * w_intra
        num = (w_inter * jnp.dot(q, c_prev.astype(BF16), preferred_element_type=F32)
               + jnp.dot(qk.astype(BF16), v, preferred_element_type=F32))
        qf = q.astype(F32)
        den = (w_inter * jnp.sum(qf * n_prev, axis=-1, keepdims=True)
               + jnp.sum(qk, axis=-1, keepdims=True))
        h_ref[:, hs] = num / jnp.maximum(jnp.abs(den), jnp.exp(-m_row))

        b_last = b_c[L - 1:L, :]
        g_r = b_last - b_r + li_r
        m_new = jnp.maximum(b_last + m_prev, jnp.max(g_r, axis=-1, keepdims=True))
        decay = jnp.exp(b_last + m_prev - m_new)
        wk_c = jnp.exp(b_last - b_c + li_c - m_new)
        kf = k.astype(F32)
        wv = (wk_c * v.astype(F32)).astype(BF16)
        c_ref[0, h] = decay * c_prev + lax.dot_general(
            k, wv, (((0,), (0,)), ((), ())), preferred_element_type=F32)
        n_ref[0, h] = decay * n_prev + jnp.sum(wk_c * kf, axis=0, keepdims=True)
        m_ref[0, h] = jnp.broadcast_to(m_new, (1, LANES))


def _mlstm_prompt(mq, mk, mv, sm, gt, n_batch, seq, chunk):
    nc = seq // chunk
    row = lambda b, c: (b * nc + c, 0)
    st = lambda b, c: (b, 0, 0, 0)
    return pl.pallas_call(
        _mlstm_kernel,
        grid=(n_batch, nc),
        in_specs=[pl.BlockSpec((chunk, 512), row)] * 3
        + [pl.BlockSpec((chunk, LANES), row), pl.BlockSpec((8, chunk), lambda b, c: (0, b * nc + c))],
        out_specs=[pl.BlockSpec((chunk, 512), row),
                   pl.BlockSpec((1, M_HEADS, M_DK, M_DV), st),
                   pl.BlockSpec((1, M_HEADS, 1, M_DK), st),
                   pl.BlockSpec((1, M_HEADS, 1, LANES), st)],
        out_shape=(jax.ShapeDtypeStruct((n_batch * seq, 512), F32),
                   jax.ShapeDtypeStruct((n_batch, M_HEADS, M_DK, M_DV), F32),
                   jax.ShapeDtypeStruct((n_batch, M_HEADS, 1, M_DK), F32),
                   jax.ShapeDtypeStruct((n_batch, M_HEADS, 1, LANES), F32)),
        compiler_params=pltpu.CompilerParams(
            dimension_semantics=("arbitrary", "arbitrary"), vmem_limit_bytes=VMEM_LIMIT),
        name="mlstm_chunkwise",
    )(mq, mk, mv, sm, gt)


def _sort_key(x):
    x = jnp.where(x == 0.0, 0.0, x)
    i = pltpu.bitcast(x, I32)
    return i ^ ((i >> 31) & 0x7FFFFFFF)


def _dsa_prompt_kernel(iq_ref, sm_ref, aq_ref, kidx_ref, k_ref, v_ref, out_ref, keys_ref,
                       *, tq, tk, topk, idx_bits):
    qb = pl.program_id(1)
    ntile = ((qb + 1) * tq + tk - 1) // tk
    qpos = qb * tq + lax.broadcasted_iota(I32, (tq, 1), 0)
    lane_k = lax.broadcasted_iota(I32, (tq, tk), 1)
    wsc = sm_ref[:, _SM_W:_SM_W + IDX_HEADS] * (IDX_HEADS ** -0.5) * (IDX_DIM ** -0.5)
    q_idx = [iq_ref[:, h * IDX_DIM:(h + 1) * IDX_DIM] for h in range(IDX_HEADS)]

    def score_tile(t, carry):
        kt = kidx_ref[0, pl.ds(pl.multiple_of(t * tk, tk), tk), :]
        acc = jnp.zeros((tq, tk), F32)
        for h in range(IDX_HEADS):
            d = lax.dot_general(q_idx[h], kt, (((1,), (1,)), ((), ())), preferred_element_type=F32)
            acc = acc + wsc[:, h:h + 1] * jnp.maximum(d, 0.0)
        kpos = t * tk + lane_k
        keys_ref[t] = jnp.where(kpos <= qpos, _sort_key(acc), INT_MIN)
        return carry

    lax.fori_loop(0, ntile, score_tile, 0)

    def count(pred):
        def body(t, acc):
            kk = keys_ref[t]
            c = jnp.where(pred(kk, t * tk + lane_k), 1.0, 0.0)
            part = c[:, 0:LANES]
            for j in range(1, tk // LANES):
                part = part + c[:, j * LANES:(j + 1) * LANES]
            return acc + part
        acc = lax.fori_loop(0, ntile, body, jnp.zeros((tq, LANES), F32))
        return jnp.sum(acc, axis=-1, keepdims=True)

    def bit_step(i, thr):
        cand = thr + lax.shift_left(jnp.int32(1), 31 - i)
        cnt = count(lambda kk, kp: kk >= cand)
        return jnp.where(cnt >= topk, cand, thr)

    thr = lax.fori_loop(0, 32, bit_step, jnp.full((tq, 1), INT_MIN, I32))
    n_gt = count(lambda kk, kp: kk > thr)
    need = topk - n_gt

    def idx_step(i, jcut):
        cand = jcut + lax.shift_left(jnp.int32(1), idx_bits - 1 - i)
        cnt = count(lambda kk, kp: (kk == thr) & (kp < cand))
        return jnp.where(cnt < need, cand, jcut)

    jcut = lax.fori_loop(0, idx_bits, idx_step, jnp.zeros((tq, 1), I32))
    jcut = jnp.where(thr == INT_MIN, -1, jcut)

    q_att = [aq_ref[:, h * A_DH:(h + 1) * A_DH] for h in range(A_HEADS)]

    def attend(t, carry):
        ms, ls, accs = carry
        kk = keys_ref[t]
        kpos = t * tk + lane_k
        sel = (kk > thr) | ((kk == thr) & (kpos <= jcut))
        start = pl.multiple_of(t * tk, tk)
        new_m, new_l, new_acc = [], [], []
        for h in range(A_HEADS):
            hs = slice(h * A_DH, (h + 1) * A_DH)
            kh = k_ref[0, pl.ds(start, tk), hs]
            vh = v_ref[0, pl.ds(start, tk), hs]
            s = lax.dot_general(q_att[h], kh, (((1,), (1,)), ((), ())), preferred_element_type=F32)
            s = jnp.where(sel, s, NEG_BIG)
            m_new = jnp.maximum(ms[h], jnp.max(s, axis=-1, keepdims=True))
            p = jnp.where(sel, jnp.exp(s - m_new), 0.0)
            alpha = jnp.exp(ms[h] - m_new)
            new_m.append(m_new)
            new_l.append(alpha * ls[h] + jnp.sum(p, axis=-1, keepdims=True))
            new_acc.append(alpha * accs[h] + jnp.dot(p.astype(BF16), vh, preferred_element_type=F32))
        return tuple(new_m), tuple(new_l), tuple(new_acc)

    init = (tuple(jnp.full((tq, 1), NEG_BIG, F32) for _ in range(A_HEADS)),
            tuple(jnp.zeros((tq, 1), F32) for _ in range(A_HEADS)),
            tuple(jnp.zeros((tq, A_DH), F32) for _ in range(A_HEADS)))
    _, ls, accs = lax.fori_loop(0, ntile, attend, init)
    for h in range(A_HEADS):
        out_ref[:, h * A_DH:(h + 1) * A_DH] = accs[h] / ls[h]


def _dsa_prompt(iq, sm, aq, ikb, akb, avb, n_batch, seq, tq, tk):
    nq = seq // tq
    topk = min(TOPK_MAX, seq // 4)
    row = lambda b, q: (b * nq + q, 0)
    full = lambda b, q: (b, 0, 0)
    kern = functools.partial(_dsa_prompt_kernel, tq=tq, tk=tk, topk=topk,
                             idx_bits=(seq - 1).bit_length())
    return pl.pallas_call(
        kern,
        grid=(n_batch, nq),
        in_specs=[pl.BlockSpec((tq, 512), row), pl.BlockSpec((tq, LANES), row),
                  pl.BlockSpec((tq, 512), row),
                  pl.BlockSpec((1, seq, IDX_DIM), full, pipeline_mode=pl.Buffered(1)),
                  pl.BlockSpec((1, seq, 512), full, pipeline_mode=pl.Buffered(1)),
                  pl.BlockSpec((1, seq, 512), full, pipeline_mode=pl.Buffered(1))],
        out_specs=pl.BlockSpec((tq, 512), row),
        out_shape=jax.ShapeDtypeStruct((n_batch * seq, 512), F32),
        scratch_shapes=[pltpu.VMEM((seq // tk, tq, tk), I32)],
        compiler_params=pltpu.CompilerParams(
            dimension_semantics=("arbitrary", "arbitrary"), vmem_limit_bytes=VMEM_LIMIT),
        name="dsa_prompt",
    )(iq, sm, aq, ikb.reshape(n_batch, seq, IDX_DIM), akb.reshape(n_batch, seq, 512),
      avb.reshape(n_batch, seq, 512))


def _out_kernel(x_ref, hm_ref, mo_ref, mz_ref, ha_ref, az_ref, gain_ref, wo_ref, g_ref, b_ref,
                y_ref, *, alpha):
    hm = hm_ref[...] * _sigmoid(mo_ref[...])
    parts = []
    for h in range(M_HEADS):
        hh = hm[:, h * M_DV:(h + 1) * M_DV]
        mu = jnp.mean(hh, axis=-1, keepdims=True)
        var = jnp.mean(jnp.square(hh - mu), axis=-1, keepdims=True)
        parts.append((hh - mu) * lax.rsqrt(var + LN_EPS))
    mz = mz_ref[...]
    az = az_ref[...]
    g_m = jnp.concatenate(parts, axis=-1) * gain_ref[...] * (mz * _sigmoid(mz))
    g_a = ha_ref[...] * (az * _sigmoid(az))
    mixed = jnp.concatenate([g_m, g_a], axis=-1).astype(BF16)
    sub = jnp.dot(mixed, wo_ref[...], preferred_element_type=F32)
    r = alpha * x_ref[...] + sub
    mu = jnp.mean(r, axis=-1, keepdims=True)
    var = jnp.mean(jnp.square(r - mu), axis=-1, keepdims=True)
    y_ref[...] = (r - mu) * lax.rsqrt(var + LN_EPS) * g_ref[...] + b_ref[...]


def _out_mix(x2d, hm, mo, mz, ha, az, gain, wo, ln_g, ln_b, alpha, tm):
    n_rows = x2d.shape[0]
    row = lambda i: (i, 0)
    return pl.pallas_call(
        functools.partial(_out_kernel, alpha=alpha),
        grid=(n_rows // tm,),
        in_specs=[pl.BlockSpec((tm, D_MODEL), row)] + [pl.BlockSpec((tm, 512), row)] * 5
        + [_const_spec((1, M_WIDTH)), _const_spec((MIX_WIDTH, D_MODEL)),
           _const_spec((1, D_MODEL)), _const_spec((1, D_MODEL))],
        out_specs=pl.BlockSpec((tm, D_MODEL), row),
        out_shape=jax.ShapeDtypeStruct((n_rows, D_MODEL), F32),
        compiler_params=pltpu.CompilerParams(
            dimension_semantics=("arbitrary",), vmem_limit_bytes=VMEM_LIMIT),
        name="out_mix",
    )(x2d, hm, mo, mz, ha, az, gain[None, :], wo.astype(BF16), ln_g[None, :], ln_b[None, :])


def _mlstm_step_kernel(qc_ref, kc_ref, qr_ref, kr_ref, vr_ref, i_ref, f_ref, c0_ref, n0_ref, m0_ref,
                       h_ref, c_ref, n_ref, m_ref):
    lf = _log_sigmoid(f_ref[...])
    m0 = m0_ref[...]
    it = i_ref[...]
    m_new = jnp.maximum(lf + m0, it)
    fg = jnp.exp(lf + m0 - m_new)
    ig = jnp.exp(it - m_new)
    c_new = fg * c0_ref[...] + (ig * kc_ref[...]) * vr_ref[...]
    n_new = fg * n0_ref[...] + ig * kr_ref[...]
    num = jnp.sum(qc_ref[...] * c_new, axis=2, keepdims=True)
    den = jnp.sum(qr_ref[...] * n_new, axis=3, keepdims=True)
    h_ref[...] = num / jnp.maximum(jnp.abs(den), jnp.exp(-m_new))
    c_ref[...] = c_new
    n_ref[...] = n_new
    m_ref[...] = m_new


def _mlstm_step(mq, mk, mv, sm, c0, n0, m0, tb):
    db = mq.shape[0]
    col = lambda t: t.reshape(db, M_HEADS, M_DK, 1)
    rowv = lambda t: t.reshape(db, M_HEADS, 1, M_DK)
    sc = lambda t: t.reshape(db, M_HEADS, 1, 1)
    idx = lambda i: (i, 0, 0, 0)
    s_col = pl.BlockSpec((tb, M_HEADS, M_DK, 1), idx)
    s_row = pl.BlockSpec((tb, M_HEADS, 1, M_DK), idx)
    s_sc = pl.BlockSpec((tb, M_HEADS, 1, 1), idx)
    s_mat = pl.BlockSpec((tb, M_HEADS, M_DK, M_DV), idx)
    return pl.pallas_call(
        _mlstm_step_kernel,
        grid=(db // tb,),
        in_specs=[s_col, s_col, s_row, s_row, s_row, s_sc, s_sc, s_mat, s_row, s_sc],
        out_specs=[s_row, s_mat, s_row, s_sc],
        out_shape=(jax.ShapeDtypeStruct((db, M_HEADS, 1, M_DV), F32),
                   jax.ShapeDtypeStruct((db, M_HEADS, M_DK, M_DV), F32),
                   jax.ShapeDtypeStruct((db, M_HEADS, 1, M_DK), F32),
                   jax.ShapeDtypeStruct((db, M_HEADS, 1, 1), F32)),
        compiler_params=pltpu.CompilerParams(
            dimension_semantics=("arbitrary",), vmem_limit_bytes=VMEM_LIMIT),
        name="mlstm_step",
    )(col(mq), col(mk), rowv(mq), rowv(mk), rowv(mv),
      sc(sm[:, _SM_I:_SM_I + M_HEADS]), sc(sm[:, _SM_F:_SM_F + M_HEADS]),
      c0, rowv(n0), sc(m0))


def _page_copies(pt_ref, src_hbm, buf, sem, seq, first_page, n_pages, slot):
    return [pltpu.make_async_copy(src_hbm.at[pt_ref[seq, first_page + j]], buf.at[slot, j], sem.at[slot])
            for j in range(n_pages)]


def _samp_index_kernel(pt_ref, qt_ref, w_ref, knew_ref, kidx_hbm, out_ref, buf, sem, *, n_pages):
    b = pl.program_id(0)
    nb = pl.num_programs(0)
    slot = b % 2

    @pl.when(b == 0)
    def _():
        for cp in _page_copies(pt_ref, kidx_hbm, buf, sem, 0, 0, n_pages, 0):
            cp.start()

    @pl.when(b + 1 < nb)
    def _():
        for cp in _page_copies(pt_ref, kidx_hbm, buf, sem, b + 1, 0, n_pages, 1 - slot):
            cp.start()

    for cp in _page_copies(pt_ref, kidx_hbm, buf, sem, b, 0, n_pages, slot):
        cp.wait()

    qt = qt_ref[0]
    wrow = w_ref[0] * (IDX_HEADS ** -0.5) * (IDX_DIM ** -0.5)
    lane = lax.broadcasted_iota(I32, (PAGE_SIZE, LANES), 1)
    row = lax.broadcasted_iota(I32, (PAGE_SIZE, LANES), 0)

    def page(p, acc):
        kp = buf[slot, p].astype(BF16)
        d = jnp.dot(kp, qt, preferred_element_type=F32)
        col = jnp.sum(jnp.maximum(d, 0.0) * wrow, axis=-1, keepdims=True)
        return jnp.where(lane == p, col, acc)

    acc = lax.fori_loop(0, n_pages, page, jnp.zeros((PAGE_SIZE, LANES), F32))
    kn = jnp.broadcast_to(knew_ref[0], (8, IDX_DIM)).astype(BF16)
    dn = jnp.dot(kn, qt, preferred_element_type=F32)
    scn = jnp.sum(jnp.maximum(dn, 0.0) * wrow, axis=-1, keepdims=True)[0:1, :]
    is_new = (lane == n_pages) & (row == 0)
    acc = jnp.where(is_new, scn, acc)
    out_ref[0] = jnp.where((lane < n_pages) | is_new, acc, -jnp.inf)


def _samp_index(page_table, qt, wrow, knew, kidx_cache):
    db, n_pages = page_table.shape
    grid_spec = pltpu.PrefetchScalarGridSpec(
        num_scalar_prefetch=1,
        grid=(db,),
        in_specs=[pl.BlockSpec((1, IDX_DIM, LANES), lambda b, pt: (b, 0, 0)),
                  pl.BlockSpec((1, 1, LANES), lambda b, pt: (b, 0, 0)),
                  pl.BlockSpec((1, 1, IDX_DIM), lambda b, pt: (b, 0, 0)),
                  pl.BlockSpec(memory_space=pl.ANY)],
        out_specs=pl.BlockSpec((1, PAGE_SIZE, LANES), lambda b, pt: (b, 0, 0)),
        scratch_shapes=[pltpu.VMEM((2, n_pages, PAGE_SIZE, IDX_DIM), F32),
                        pltpu.SemaphoreType.DMA((2,))],
    )
    return pl.pallas_call(
        functools.partial(_samp_index_kernel, n_pages=n_pages),
        grid_spec=grid_spec,
        out_shape=jax.ShapeDtypeStruct((db, PAGE_SIZE, LANES), F32),
        compiler_params=pltpu.CompilerParams(
            dimension_semantics=("arbitrary",), vmem_limit_bytes=VMEM_LIMIT),
        name="samp_index",
    )(page_table, qt, wrow, knew, kidx_cache)


def _samp_topk_kernel(sc_ref, mask_ref, *, topk, idx_bits):
    sc = sc_ref[...]
    keys = jnp.where(sc == -jnp.inf, INT_MIN, _sort_key(sc))
    kpos = (lax.broadcasted_iota(I32, sc.shape, 2) * PAGE_SIZE
            + lax.broadcasted_iota(I32, sc.shape, 1))

    def count(m):
        c = jnp.sum(jnp.where(m, 1.0, 0.0), axis=1, keepdims=True)
        return jnp.sum(c, axis=2, keepdims=True)

    def bit_step(i, thr):
        cand = thr + lax.shift_left(jnp.int32(1), 31 - i)
        return jnp.where(count(keys >= cand) >= topk, cand, thr)

    thr = lax.fori_loop(0, 32, bit_step, jnp.full((sc.shape[0], 1, 1), INT_MIN, I32))
    need = topk - count(keys > thr)

    def idx_step(i, jcut):
        cand = jcut + lax.shift_left(jnp.int32(1), idx_bits - 1 - i)
        return jnp.where(count((keys == thr) & (kpos < cand)) < need, cand, jcut)

    jcut = lax.fori_loop(0, idx_bits, idx_step, jnp.zeros((sc.shape[0], 1, 1), I32))
    jcut = jnp.where(thr == INT_MIN, -1, jcut)
    sel = (keys > thr) | ((keys == thr) & (kpos <= jcut))
    mask_ref[...] = jnp.where(sel, 1.0, 0.0)


def _samp_topk(sc, n_keys, tb):
    db = sc.shape[0]
    topk = min(TOPK_MAX, n_keys // 4)
    spec = pl.BlockSpec((tb, PAGE_SIZE, LANES), lambda i: (i, 0, 0))
    return pl.pallas_call(
        functools.partial(_samp_topk_kernel, topk=topk, idx_bits=(n_keys - 1).bit_length()),
        grid=(db // tb,),
        in_specs=[spec], out_specs=spec,
        out_shape=jax.ShapeDtypeStruct(sc.shape, F32),
        compiler_params=pltpu.CompilerParams(
            dimension_semantics=("arbitrary",), vmem_limit_bytes=VMEM_LIMIT),
        name="samp_topk",
    )(sc)


def _samp_attn_kernel(pt_ref, qc_ref, qr_ref, mask_ref, knew_ref, vnew_ref, k_hbm, v_hbm, out_ref,
                      kbuf, vbuf, sem, m_ref, l_ref, acc_ref, *, n_pages, pg):
    b = pl.program_id(0)
    c = pl.program_id(1)
    nb = pl.num_programs(0)
    nch = pl.num_programs(1)
    g = b * nch + c
    slot = g % 2

    def copies(seq, chunk, sl):
        return (_page_copies(pt_ref, k_hbm, kbuf, sem.at[0], seq, chunk * pg, pg, sl)
                + _page_copies(pt_ref, v_hbm, vbuf, sem.at[1], seq, chunk * pg, pg, sl))

    @pl.when(g == 0)
    def _():
        for cp in copies(0, 0, 0):
            cp.start()

    @pl.when(g + 1 < nb * nch)
    def _():
        last = c + 1 == nch
        nseq = jnp.where(last, b + 1, b)
        nchunk = jnp.where(last, 0, c + 1)
        for cp in copies(nseq, nchunk, 1 - slot):
            cp.start()

    @pl.when(c == 0)
    def _():
        m_ref[...] = jnp.full_like(m_ref, NEG_BIG)
        l_ref[...] = jnp.zeros_like(l_ref)
        acc_ref[...] = jnp.zeros_like(acc_ref)

    for cp in copies(b, c, slot):
        cp.wait()

    lane = lax.broadcasted_iota(I32, (PAGE_SIZE, LANES), 1)
    qrow_id = lax.broadcasted_iota(I32, (A_WIDTH, LANES), 0) // A_DH
    qlane = lax.broadcasted_iota(I32, (A_WIDTH, LANES), 1)
    qbd = jnp.where(qlane == qrow_id, qc_ref[0], 0.0).astype(BF16)
    maskmat = mask_ref[0]

    s_list, sel_list = [], []
    mx = jnp.full((1, LANES), NEG_BIG, F32)
    for j in range(pg):
        kp = kbuf[slot, j].astype(BF16)
        s = jnp.dot(kp, qbd, preferred_element_type=F32)
        selc = jnp.sum(jnp.where(lane == c * pg + j, maskmat, 0.0), axis=-1, keepdims=True) > 0.0
        s = jnp.where(selc, s, NEG_BIG)
        mx = jnp.maximum(mx, jnp.max(s, axis=0, keepdims=True))
        s_list.append(s)
        sel_list.append(selc)
    m_old = m_ref[...]
    m_new = jnp.maximum(m_old, mx)
    alpha = jnp.exp(m_old - m_new)
    lsum = jnp.zeros((1, LANES), F32)
    contrib = [jnp.zeros((PAGE_SIZE, A_DH), F32) for _ in range(A_HEADS)]
    for j in range(pg):
        p = jnp.where(sel_list[j], jnp.exp(s_list[j] - m_new), 0.0)
        lsum = lsum + jnp.sum(p, axis=0, keepdims=True)
        vp = vbuf[slot, j]
        for h in range(A_HEADS):
            contrib[h] = contrib[h] + p[:, h:h + 1] * vp[:, h * A_DH:(h + 1) * A_DH]
    m_ref[...] = m_new
    l_ref[...] = alpha * l_ref[...] + lsum
    for h in range(A_HEADS):
        acc_ref[h] = alpha[:, h:h + 1] * acc_ref[h] + contrib[h]

    @pl.when(c + 1 == nch)
    def _():
        qr = qr_ref[0]
        kn = knew_ref[0]
        vn = vnew_ref[0]
        lane1 = lax.broadcasted_iota(I32, (1, LANES), 1)
        snew = jnp.full((1, LANES), NEG_BIG, F32)
        for h in range(A_HEADS):
            hs = slice(h * A_DH, (h + 1) * A_DH)
            sh = jnp.sum(qr[:, hs] * kn[:, hs], axis=-1, keepdims=True)
            snew = jnp.where(lane1 == h, sh, snew)
        sel_new = maskmat[0:1, n_pages:n_pages + 1] > 0.0
        m_cur = m_ref[...]
        m_fin = jnp.where(sel_new, jnp.maximum(m_cur, snew), m_cur)
        a_fin = jnp.exp(m_cur - m_fin)
        p_new = jnp.where(sel_new, jnp.exp(snew - m_fin), 0.0)
        l_fin = a_fin * l_ref[...] + p_new
        for h in range(A_HEADS):
            hs = slice(h * A_DH, (h + 1) * A_DH)
            past = a_fin[:, h:h + 1] * jnp.sum(acc_ref[h], axis=0, keepdims=True)
            out_ref[0, :, hs] = (past + p_new[:, h:h + 1] * vn[:, hs]) / l_fin[:, h:h + 1]


def _samp_attn(page_table, qcol, qrow, mask, knew, vnew, k_cache, v_cache, pg):
    db, n_pages = page_table.shape
    nch = n_pages // pg
    per_seq = lambda shape: pl.BlockSpec((1,) + shape, lambda b, c, pt: (b, 0, 0))
    grid_spec = pltpu.PrefetchScalarGridSpec(
        num_scalar_prefetch=1,
        grid=(db, nch),
        in_specs=[per_seq((A_WIDTH, 1)), per_seq((1, A_WIDTH)), per_seq((PAGE_SIZE, LANES)),
                  per_seq((1, A_WIDTH)), per_seq((1, A_WIDTH)),
                  pl.BlockSpec(memory_space=pl.ANY), pl.BlockSpec(memory_space=pl.ANY)],
        out_specs=per_seq((1, A_WIDTH)),
        scratch_shapes=[pltpu.VMEM((2, pg, PAGE_SIZE, A_WIDTH), F32),
                        pltpu.VMEM((2, pg, PAGE_SIZE, A_WIDTH), F32),
                        pltpu.SemaphoreType.DMA((2, 2)),
                        pltpu.VMEM((1, LANES), F32), pltpu.VMEM((1, LANES), F32),
                        pltpu.VMEM((A_HEADS, PAGE_SIZE, A_DH), F32)],
    )
    return pl.pallas_call(
        functools.partial(_samp_attn_kernel, n_pages=n_pages, pg=pg),
        grid_spec=grid_spec,
        out_shape=jax.ShapeDtypeStruct((db, 1, A_WIDTH), F32),
        compiler_params=pltpu.CompilerParams(
            dimension_semantics=("arbitrary", "arbitrary"), vmem_limit_bytes=VMEM_LIMIT),
        name="samp_attn",
    )(page_table, qcol, qrow, mask, knew, vnew, k_cache, v_cache)


def kernel(x_prompt, x_sample, state_C, state_n, state_m, cache_k, cache_v, cache_kidx, page_table,
           w_in, b_in, mh_gain, w_out, ln_g, ln_b):
    depth = w_in.shape[0]
    n_batch, seq, _ = x_prompt.shape
    db, dec_seq, _ = x_sample.shape
    assert dec_seq == 1, "one new token per sampled sequence"
    n_pages = page_table.shape[1]
    past = n_pages * PAGE_SIZE
    alpha = (2.0 * depth) ** 0.25
    pos_p = jnp.arange(seq)
    pos_s = past + jnp.arange(dec_seq)
    tm = min(512, seq)
    chunk = min(256, seq)
    tq, tk = 128, min(512, seq)

    hp = x_prompt.reshape(n_batch * seq, D_MODEL)
    hs = x_sample.reshape(db * dec_seq, D_MODEL)
    outs = {k: [] for k in ("kp", "vp", "ip", "Cp", "np", "mp", "ks", "vs", "is", "Cs", "ns", "ms")}
    for l in range(depth):
        weights = _prep_weights(w_in[l], b_in[l])
        (mq, mk, mv, mo, mz, aq, ak, av, az, iq, ik, sm, gt, akb, avb, ikb) = _project(
            hp, pos_p, weights, n_batch, tm, BF16)
        h_m, c_p, n_p, m_p = _mlstm_prompt(mq, mk, mv, sm, gt, n_batch, seq, chunk)
        h_a = _dsa_prompt(iq, sm, aq, ikb, akb, avb, n_batch, seq, tq, tk)
        hp_new = _out_mix(hp, h_m, mo, mz, h_a, az, mh_gain[l], w_out[l], ln_g[l], ln_b[l], alpha, tm)
        outs["kp"].append(ak.reshape(n_batch, seq, A_HEADS, A_DH))
        outs["vp"].append(av.reshape(n_batch, seq, A_HEADS, A_DH))
        outs["ip"].append(ik.reshape(n_batch, seq, IDX_DIM))
        outs["Cp"].append(c_p)
        outs["np"].append(n_p.reshape(n_batch, M_HEADS, M_DK))
        outs["mp"].append(m_p[:, :, 0, 0])
        hp = hp_new
        (smq, smk, smv, smo, smz, saq, sak, sav, saz, siq, sik, ssm, _, _, _, _) = _project(
            hs, jnp.tile(pos_s, db), weights, 1, min(128, db * dec_seq), F32)
        g_m, c_s, n_s, m_s = _mlstm_step(smq, smk, smv, ssm, state_C[l], state_n[l], state_m[l],
                                         tb=min(8, db))
        qt = jnp.pad(siq.reshape(db, IDX_HEADS, IDX_DIM).transpose(0, 2, 1),
                     ((0, 0), (0, 0), (0, LANES - IDX_HEADS)))
        wrow = jnp.pad(ssm[:, _SM_W:_SM_W + IDX_HEADS], ((0, 0), (0, LANES - IDX_HEADS)))[:, None, :]
        sc = _samp_index(page_table, qt, wrow, sik[:, None, :], cache_kidx[l])
        mask = _samp_topk(sc, past + dec_seq, tb=min(8, db))
        saq32 = saq.astype(F32)
        n_pool = cache_k.shape[1]
        g_a = _samp_attn(page_table, saq32[:, :, None], saq32[:, None, :], mask,
                         sak[:, None, :], sav[:, None, :],
                         cache_k[l].reshape(n_pool, PAGE_SIZE, A_WIDTH),
                         cache_v[l].reshape(n_pool, PAGE_SIZE, A_WIDTH), pg=min(8, n_pages))
        hs_new = _out_mix(hs, g_m.reshape(db, M_WIDTH), smo, smz, g_a.reshape(db, A_WIDTH), saz,
                          mh_gain[l], w_out[l], ln_g[l], ln_b[l], alpha, min(128, db))
        outs["ks"].append(sak.reshape(db, dec_seq, A_HEADS, A_DH))
        outs["vs"].append(sav.reshape(db, dec_seq, A_HEADS, A_DH))
        outs["is"].append(sik.reshape(db, dec_seq, IDX_DIM))
        outs["Cs"].append(c_s)
        outs["ns"].append(n_s.reshape(db, M_HEADS, M_DK))
        outs["ms"].append(m_s.reshape(db, M_HEADS))
        hs = hs_new
    st = lambda k: jnp.stack(outs[k])
    return (hp.reshape(n_batch, seq, D_MODEL), hs.reshape(db, dec_seq, D_MODEL),
            st("kp"), st("vp"), st("ip"), st("Cp"), st("np"), st("mp"),
            st("ks"), st("vs"), st("is"), st("Cs"), st("ns"), st("ms"))
```

```python
import functools

import jax
import jax.numpy as jnp
from jax import lax
from jax.experimental import pallas as pl
from jax.experimental.pallas import tpu as pltpu

F32 = jnp.float32
BF16 = jnp.bfloat16
I32 = jnp.int32

D_MODEL = 1024
PAGE_SIZE = 128
M_HEADS = 4
M_DK = 128
M_DV = 128
M_WIDTH = M_HEADS * M_DV
A_HEADS = 4
A_DH = 128
A_WIDTH = A_HEADS * A_DH
ROT_DIM = A_DH // 4
ROPE_THETA = 500000.0
IDX_HEADS = 8
IDX_DIM = 64
IDX_ROT = IDX_DIM // 4
TOPK_MAX = 256
MIX_WIDTH = M_WIDTH + A_WIDTH
LN_EPS = 1e-5

LANES = 128
NEG_BIG = -1e30
INT_MIN = -(2 ** 31)

_SPLITS = (
    ("m_q", M_WIDTH), ("m_k", M_WIDTH), ("m_v", M_WIDTH), ("m_o", M_WIDTH), ("m_z", M_WIDTH),
    ("m_i", M_HEADS), ("m_f", M_HEADS),
    ("a_q", A_WIDTH), ("a_k", A_WIDTH), ("a_v", A_WIDTH), ("a_z", A_WIDTH),
    ("i_q", IDX_HEADS * IDX_DIM), ("i_k", IDX_DIM), ("i_w", IDX_HEADS),
)
_BIG = ("m_q", "m_k", "m_v", "m_o", "m_z", "a_q", "a_k", "a_v", "a_z", "i_q")
_SM_I = IDX_DIM
_SM_F = IDX_DIM + M_HEADS
_SM_W = IDX_DIM + 2 * M_HEADS
VMEM_LIMIT = 56 * 1024 * 1024


def _offsets():
    offs, start = {}, 0
    for name, n in _SPLITS:
        offs[name] = (start, start + n)
        start += n
    return offs


def _log_sigmoid(x):
    return jnp.minimum(x, 0.0) - jnp.log1p(jnp.exp(-jnp.abs(x)))


def _sigmoid(x):
    return 1.0 / (1.0 + jnp.exp(-x))


def _rope_tables(pos, rot_dim, period):
    half = rot_dim // 2
    inv = ROPE_THETA ** (-jnp.arange(half, dtype=F32) / half)
    ang = pos.astype(F32)[:, None] * inv
    cos, sin = jnp.cos(ang), jnp.sin(ang)
    n = pos.shape[0]
    c = jnp.concatenate([cos, cos, jnp.ones((n, period - rot_dim), F32)], axis=-1)
    s = jnp.concatenate([sin, sin, jnp.zeros((n, period - rot_dim), F32)], axis=-1)
    reps = LANES // period
    return jnp.tile(c, (1, reps)), jnp.tile(s, (1, reps))


def _rope128(x, c, s, half, period):
    lane = lax.broadcasted_iota(I32, x.shape, 1) % period
    up = pltpu.roll(x, LANES - half, axis=1)
    dn = pltpu.roll(x, half, axis=1)
    t = jnp.where(lane < half, -up, dn)
    return x * c + t * s


def _proj_kernel(x_ref, wb_ref, ws_ref, wg_ref, bb_ref, bs_ref, bg_ref,
                 ch_ref, sh_ref, ci_ref, si_ref,
                 mq_ref, mk_ref, mv_ref, mo_ref, mz_ref, aq_ref, ak_ref, av_ref, az_ref,
                 iq_ref, ik_ref, sm_ref, gt_ref, akb_ref, avb_ref, ikb_ref):
    x = x_ref[...].astype(BF16)

    def piece(j):
        w = wb_ref[:, j * 512:(j + 1) * 512]
        return jnp.dot(x, w, preferred_element_type=F32) + bb_ref[:, j * 512:(j + 1) * 512]

    def rope512(p, c, s, half, period):
        return jnp.concatenate(
            [_rope128(p[:, g * LANES:(g + 1) * LANES], c, s, half, period) for g in range(4)], axis=-1)

    ch, sh = ch_ref[...], sh_ref[...]
    ci, si = ci_ref[...], si_ref[...]
    mq_ref[...] = piece(0).astype(mq_ref.dtype)
    mk_ref[...] = (piece(1) * (M_DK ** -0.5)).astype(mk_ref.dtype)
    mv_ref[...] = piece(2).astype(mv_ref.dtype)
    mo_ref[...] = piece(3)
    mz_ref[...] = piece(4)
    aq = rope512(piece(5), ch, sh, ROT_DIM // 2, LANES)
    aq_ref[...] = (aq * (A_DH ** -0.5)).astype(BF16)
    ak = rope512(piece(6), ch, sh, ROT_DIM // 2, LANES)
    ak_ref[...] = ak
    akb_ref[...] = ak.astype(BF16)
    av = piece(7)
    av_ref[...] = av
    avb_ref[...] = av.astype(BF16)
    az_ref[...] = piece(8)
    iq_ref[...] = rope512(piece(9), ci, si, IDX_ROT // 2, IDX_DIM).astype(BF16)
    psm = jnp.dot(x, ws_ref[...], preferred_element_type=F32) + bs_ref[...]
    sm_ref[...] = psm
    ik = _rope128(psm, ci, si, IDX_ROT // 2, IDX_DIM)[:, :IDX_DIM]
    ik_ref[...] = ik
    ikb_ref[...] = ik.astype(BF16)
    gt = lax.dot_general(wg_ref[...], x, (((1,), (1,)), ((), ())), preferred_element_type=F32)
    gt_ref[...] = gt + bg_ref[...]


def _const_spec(shape):
    nd = len(shape)
    return pl.BlockSpec(shape, lambda *_: (0,) * nd, pipeline_mode=pl.Buffered(1))


def _project(x2d, pos, weights, n_batch, tm, qkv_dtype):
    wb, ws, wg, bb, bs, bg = weights
    n_rows = x2d.shape[0]
    n_pos = pos.shape[0]
    npb = n_pos // tm
    ch, sh = _rope_tables(pos, ROT_DIM, LANES)
    ci, si = _rope_tables(pos, IDX_ROT, IDX_DIM)

    row = lambda p, b: (b * npb + p, 0)
    tab = lambda p, b: (p, 0)
    wide = lambda dt: jax.ShapeDtypeStruct((n_rows, 512), dt)
    out_shape = (
        wide(qkv_dtype), wide(qkv_dtype), wide(qkv_dtype), wide(F32), wide(F32),
        wide(BF16), wide(F32), wide(F32), wide(F32),
        wide(BF16),
        jax.ShapeDtypeStruct((n_rows, IDX_DIM), F32),
        jax.ShapeDtypeStruct((n_rows, LANES), F32),
        jax.ShapeDtypeStruct((8, n_rows), F32),
        wide(BF16), wide(BF16),
        jax.ShapeDtypeStruct((n_rows, IDX_DIM), BF16),
    )
    out_specs = (
        [pl.BlockSpec((tm, 512), row)] * 10
        + [pl.BlockSpec((tm, IDX_DIM), row), pl.BlockSpec((tm, LANES), row),
           pl.BlockSpec((8, tm), lambda p, b: (0, b * npb + p)),
           pl.BlockSpec((tm, 512), row), pl.BlockSpec((tm, 512), row),
           pl.BlockSpec((tm, IDX_DIM), row)]
    )
    in_specs = [
        pl.BlockSpec((tm, D_MODEL), row),
        _const_spec(wb.shape), _const_spec(ws.shape), _const_spec(wg.shape),
        _const_spec(bb.shape), _const_spec(bs.shape), _const_spec(bg.shape),
        pl.BlockSpec((tm, LANES), tab), pl.BlockSpec((tm, LANES), tab),
        pl.BlockSpec((tm, LANES), tab), pl.BlockSpec((tm, LANES), tab),
    ]
    return pl.pallas_call(
        _proj_kernel,
        grid=(npb, n_batch),
        in_specs=in_specs,
        out_specs=out_specs,
        out_shape=out_shape,
        compiler_params=pltpu.CompilerParams(
            dimension_semantics=("arbitrary", "arbitrary"), vmem_limit_bytes=VMEM_LIMIT),
        name="proj_rope",
    )(x2d, wb, ws, wg, bb, bs, bg, ch, sh, ci, si)


def _prep_weights(w_in, b_in):
    offs = _offsets()
    col = lambda nm: w_in[:, offs[nm][0]:offs[nm][1]]
    bcol = lambda nm: b_in[offs[nm][0]:offs[nm][1]]
    wb = jnp.concatenate([col(nm) for nm in _BIG], axis=1).astype(BF16)
    bb = jnp.concatenate([bcol(nm) for nm in _BIG])[None, :]
    pad = LANES - (IDX_DIM + 2 * M_HEADS + IDX_HEADS)
    ws = jnp.concatenate([col("i_k"), col("m_i"), col("m_f"), col("i_w"),
                          jnp.zeros((D_MODEL, pad), w_in.dtype)], axis=1).astype(BF16)
    bs = jnp.concatenate([bcol("i_k"), bcol("m_i"), bcol("m_f"), bcol("i_w"),
                          jnp.zeros((pad,), b_in.dtype)])[None, :]
    wg = jnp.concatenate([col("m_i"), col("m_f")], axis=1).T.astype(BF16)
    bg = jnp.concatenate([bcol("m_i"), bcol("m_f")])[:, None]
    return wb, ws, wg, bb, bs, bg


def _cumsum_rows(x):
    n = x.shape[0]
    row = lax.broadcasted_iota(I32, x.shape, 0)
    s = 1
    while s < n:
        x = x + jnp.where(row >= s, pltpu.roll(x, s, axis=0), 0.0)
        s *= 2
    return x


def _cumsum_lanes(x):
    n = x.shape[1]
    lane = lax.broadcasted_iota(I32, x.shape, 1)
    s = 1
    while s < n:
        x = x + jnp.where(lane >= s, pltpu.roll(x, s, axis=1), 0.0)
        s *= 2
    return x


def _mlstm_kernel(mq_ref, mk_ref, mv_ref, sm_ref, gt_ref, h_ref, c_ref, n_ref, m_ref):
    c_idx = pl.program_id(1)
    L = mq_ref.shape[0]

    @pl.when(c_idx == 0)
    def _():
        c_ref[...] = jnp.zeros_like(c_ref)
        n_ref[...] = jnp.zeros_like(n_ref)
        m_ref[...] = jnp.zeros_like(m_ref)

    sm = sm_ref[...]
    gt = gt_ref[...]
    bcol = _cumsum_rows(_log_sigmoid(sm))
    brow = _cumsum_lanes(_log_sigmoid(gt))
    tpos = lax.broadcasted_iota(I32, (L, L), 0)
    spos = lax.broadcasted_iota(I32, (L, L), 1)
    causal = tpos >= spos

    for h in range(M_HEADS):
        hs = slice(h * M_DK, (h + 1) * M_DK)
        q = mq_ref[:, hs]
        k = mk_ref[:, hs]
        v = mv_ref[:, hs]
        li_c = sm[:, _SM_I + h:_SM_I + h + 1]
        b_c = bcol[:, _SM_F + h:_SM_F + h + 1]
        li_r = gt[h:h + 1, :]
        b_r = brow[M_HEADS + h:M_HEADS + h + 1, :]
        m_prev = m_ref[0, h][:, :1]
        c_prev = c_ref[0, h]
        n_prev = n_ref[0, h]

        dmat = jnp.where(causal, b_c + (li_r - b_r), -jnp.inf)
        inter = b_c + m_prev
        m_row = jnp.maximum(inter, jnp.max(dmat, axis=-1, keepdims=True))
        w_intra = jnp.exp(dmat - m_row)
        w_inter = jnp.exp(inter - m_row)
        s = lax.dot_general(q, k, (((1,), (1,)), ((), ())), preferred_element_type=F32)
        qk = s * w_intra
        num = (w_inter * jnp.dot(q, c_prev.astype(BF16), preferred_element_type=F32)
               + jnp.dot(qk.astype(BF16), v, preferred_element_type=F32))
        qf = q.astype(F32)
        den = (w_inter * jnp.sum(qf * n_prev, axis=-1, keepdims=True)
               + jnp.sum(qk, axis=-1, keepdims=True))
        h_ref[:, hs] = num / jnp.maximum(jnp.abs(den), jnp.exp(-m_row))

        b_last = b_c[L - 1:L, :]
        g_r = b_last - b_r + li_r
        m_new = jnp.maximum(b_last + m_prev, jnp.max(g_r, axis=-1, keepdims=True))
        decay = jnp.exp(b_last + m_prev - m_new)
        wk_c = jnp.exp(b_last - b_c + li_c - m_new)
        kf = k.astype(F32)
        wv = (wk_c * v.astype(F32)).astype(BF16)
        c_ref[0, h] = decay * c_prev + lax.dot_general(
            k, wv, (((0,), (0,)), ((), ())), preferred_element_type=F32)
        n_ref[0, h] = decay * n_prev + jnp.sum(wk_c * kf, axis=0, keepdims=True)
        m_ref[0, h] = jnp.broadcast_to(m_new, (1, LANES))


def _mlstm_prompt(mq, mk, mv, sm, gt, n_batch, seq, chunk):
    nc = seq // chunk
    row = lambda b, c: (b * nc + c, 0)
    st = lambda b, c: (b, 0, 0, 0)
    return pl.pallas_call(
        _mlstm_kernel,
        grid=(n_batch, nc),
        in_specs=[pl.BlockSpec((chunk, 512), row)] * 3
        + [pl.BlockSpec((chunk, LANES), row), pl.BlockSpec((8, chunk), lambda b, c: (0, b * nc + c))],
        out_specs=[pl.BlockSpec((chunk, 512), row),
                   pl.BlockSpec((1, M_HEADS, M_DK, M_DV), st),
                   pl.BlockSpec((1, M_HEADS, 1, M_DK), st),
                   pl.BlockSpec((1, M_HEADS, 1, LANES), st)],
        out_shape=(jax.ShapeDtypeStruct((n_batch * seq, 512), F32),
                   jax.ShapeDtypeStruct((n_batch, M_HEADS, M_DK, M_DV), F32),
                   jax.ShapeDtypeStruct((n_batch, M_HEADS, 1, M_DK), F32),
                   jax.ShapeDtypeStruct((n_batch, M_HEADS, 1, LANES), F32)),
        compiler_params=pltpu.CompilerParams(
            dimension_semantics=("arbitrary", "arbitrary"), vmem_limit_bytes=VMEM_LIMIT),
        name="mlstm_chunkwise",
    )(mq, mk, mv, sm, gt)


def _key_to_float(key):
    return pltpu.bitcast(key ^ ((key >> 31) & 0x7FFFFFFF), F32)


def _kth_largest(count_ge, shape, topk):
    def bit_step(i, key):
        cand = key + lax.shift_left(jnp.int32(1), 31 - i)
        return jnp.where(count_ge(_key_to_float(cand)) >= topk, cand, key)

    thr = _key_to_float(lax.fori_loop(0, 32, bit_step, jnp.full(shape, INT_MIN, I32)))
    return jnp.where(thr > -jnp.inf, thr, -jnp.inf)


def _dsa_prompt_kernel(iq_ref, sm_ref, aq_ref, kidx_ref, k_ref, v_ref, out_ref,
                       sc_ref, wslab_ref, m_ref, l_ref, acc_ref, *, tq, tk, topk, idx_bits):
    qb = pl.program_id(1)
    ntile = ((qb + 1) * tq + tk - 1) // tk
    qpos = qb * tq + lax.broadcasted_iota(I32, (tq, 1), 0)
    lane_k = lax.broadcasted_iota(I32, (tq, tk), 1)
    wsc = sm_ref[:, _SM_W:_SM_W + IDX_HEADS] * (IDX_HEADS ** -0.5) * (IDX_DIM ** -0.5)
    q_idx = [iq_ref[:, h * IDX_DIM:(h + 1) * IDX_DIM] for h in range(IDX_HEADS)]

    ncol = tk // LANES
    cols = [slice(j * LANES, (j + 1) * LANES) for j in range(ncol)]
    for h in range(IDX_HEADS):
        wslab_ref[h] = jnp.broadcast_to(wsc[:, h:h + 1], (tq, LANES))

    def score_tile(t, carry):
        kt = kidx_ref[0, pl.ds(pl.multiple_of(t * tk, tk), tk), :]
        accs = [jnp.zeros((tq, LANES), F32) for _ in cols]
        for h in range(IDX_HEADS):
            d = lax.dot_general(q_idx[h], kt, (((1,), (1,)), ((), ())), preferred_element_type=F32)
            wh = wslab_ref[h]
            accs = [acc + wh * jnp.maximum(d[:, c], 0.0) for acc, c in zip(accs, cols)]
        kpos = t * tk + lane_k
        sc_ref[t] = jnp.where(kpos <= qpos, jnp.concatenate(accs, axis=1), -jnp.inf)
        return carry

    lax.fori_loop(0, ntile, score_tile, 0)

    def count(preds):
        def body(t, accs):
            sc = sc_ref[t]
            kpos = t * tk + lane_k
            out = []
            for pred, acc in zip(preds, accs):
                c = jnp.where(pred(sc, kpos), 1.0, 0.0)
                part = c[:, 0:LANES]
                for j in range(1, tk // LANES):
                    part = part + c[:, j * LANES:(j + 1) * LANES]
                out.append(acc + part)
            return tuple(out)
        accs = lax.fori_loop(0, ntile, body, tuple(jnp.zeros((tq, LANES), F32) for _ in preds))
        return [jnp.sum(acc, axis=-1, keepdims=True) for acc in accs]

    thr = _kth_largest(lambda c: count([lambda sc, kp: sc >= c])[0], (tq, 1), topk)
    n_gt, n_ge = count([lambda sc, kp: sc > thr, lambda sc, kp: sc >= thr])
    need = topk - n_gt
    cut_ties = jnp.max(jnp.where(thr > -jnp.inf, n_ge - topk, 1.0)) > 0.0

    def bias_simple():
        def body(t, carry):
            sc_ref[t] = jnp.where(sc_ref[t] >= thr, 0.0, NEG_BIG)
            return carry
        lax.fori_loop(0, ntile, body, 0)

    def bias_cut_ties():
        def idx_step(i, jcut):
            cand = jcut + lax.shift_left(jnp.int32(1), idx_bits - 1 - i)
            cnt, = count([lambda sc, kp: (sc == thr) & (kp < cand)])
            return jnp.where(cnt < need, cand, jcut)

        jcut = lax.fori_loop(0, idx_bits, idx_step, jnp.zeros((tq, 1), I32))
        jcut = jnp.where(thr > -jnp.inf, jcut, -1)

        def body(t, carry):
            sc = sc_ref[t]
            sel = (sc > thr) | ((sc == thr) & (t * tk + lane_k <= jcut))
            sc_ref[t] = jnp.where(sel, 0.0, NEG_BIG)
            return carry
        lax.fori_loop(0, ntile, body, 0)

    lax.cond(cut_ties, bias_cut_ties, bias_simple)

    def scores(t, h):
        hs = slice(h * A_DH, (h + 1) * A_DH)
        kh = k_ref[0, pl.ds(pl.multiple_of(t * tk, tk), tk), hs]
        s = lax.dot_general(aq_ref[:, hs], kh, (((1,), (1,)), ((), ())), preferred_element_type=F32)
        return s + sc_ref[t]

    m_ref[...] = jnp.full_like(m_ref, NEG_BIG)
    l_ref[...] = jnp.zeros_like(l_ref)
    acc_ref[...] = jnp.zeros_like(acc_ref)

    def max_sweep(t, carry):
        for h in range(A_HEADS):
            s = scores(t, h)
            m = m_ref[h]
            for c in cols:
                m = jnp.maximum(m, s[:, c])
            m_ref[h] = m
        return carry

    lax.fori_loop(0, ntile, max_sweep, 0)
    for h in range(A_HEADS):
        m_ref[h] = jnp.broadcast_to(jnp.max(m_ref[h], axis=-1, keepdims=True), (tq, LANES))

    def sum_sweep(t, carry):
        start = pl.multiple_of(t * tk, tk)
        for h in range(A_HEADS):
            s = scores(t, h)
            mb = m_ref[h]
            ps = [jnp.exp(s[:, c] - mb) for c in cols]
            lsum = ps[0]
            for pj in ps[1:]:
                lsum = lsum + pj
            l_ref[h] = l_ref[h] + lsum
            vh = v_ref[0, pl.ds(start, tk), h * A_DH:(h + 1) * A_DH]
            p = jnp.concatenate(ps, axis=1).astype(BF16)
            acc_ref[h] = acc_ref[h] + jnp.dot(p, vh, preferred_element_type=F32)
        return carry

    lax.fori_loop(0, ntile, sum_sweep, 0)
    for h in range(A_HEADS):
        out_ref[:, h * A_DH:(h + 1) * A_DH] = acc_ref[h] / jnp.sum(l_ref[h], axis=-1, keepdims=True)


def _dsa_prompt(iq, sm, aq, ikb, akb, avb, n_batch, seq, tq, tk):
    nq = seq // tq
    topk = min(TOPK_MAX, seq // 4)
    row = lambda b, q: (b * nq + q, 0)
    full = lambda b, q: (b, 0, 0)
    kern = functools.partial(_dsa_prompt_kernel, tq=tq, tk=tk, topk=topk,
                             idx_bits=(seq - 1).bit_length())
    return pl.pallas_call(
        kern,
        grid=(n_batch, nq),
        in_specs=[pl.BlockSpec((tq, 512), row), pl.BlockSpec((tq, LANES), row),
                  pl.BlockSpec((tq, 512), row),
                  pl.BlockSpec((1, seq, IDX_DIM), full, pipeline_mode=pl.Buffered(1)),
                  pl.BlockSpec((1, seq, 512), full, pipeline_mode=pl.Buffered(1)),
                  pl.BlockSpec((1, seq, 512), full, pipeline_mode=pl.Buffered(1))],
        out_specs=pl.BlockSpec((tq, 512), row),
        out_shape=jax.ShapeDtypeStruct((n_batch * seq, 512), F32),
        scratch_shapes=[pltpu.VMEM((seq // tk, tq, tk), F32),
                        pltpu.VMEM((IDX_HEADS, tq, LANES), F32),
                        pltpu.VMEM((A_HEADS, tq, LANES), F32), pltpu.VMEM((A_HEADS, tq, LANES), F32),
                        pltpu.VMEM((A_HEADS, tq, A_DH), F32)],
        compiler_params=pltpu.CompilerParams(
            dimension_semantics=("arbitrary", "arbitrary"), vmem_limit_bytes=VMEM_LIMIT),
        name="dsa_prompt",
    )(iq, sm, aq, ikb.reshape(n_batch, seq, IDX_DIM), akb.reshape(n_batch, seq, 512),
      avb.reshape(n_batch, seq, 512))


def _out_kernel(x_ref, hm_ref, mo_ref, mz_ref, ha_ref, az_ref, gain_ref, wo_ref, g_ref, b_ref,
                y_ref, *, alpha):
    hm = hm_ref[...] * _sigmoid(mo_ref[...])
    parts = []
    for h in range(M_HEADS):
        hh = hm[:, h * M_DV:(h + 1) * M_DV]
        mu = jnp.mean(hh, axis=-1, keepdims=True)
        var = jnp.mean(jnp.square(hh - mu), axis=-1, keepdims=True)
        parts.append((hh - mu) * lax.rsqrt(var + LN_EPS))
    mz = mz_ref[...]
    az = az_ref[...]
    g_m = jnp.concatenate(parts, axis=-1) * gain_ref[...] * (mz * _sigmoid(mz))
    g_a = ha_ref[...] * (az * _sigmoid(az))
    mixed = jnp.concatenate([g_m, g_a], axis=-1).astype(BF16)
    sub = jnp.dot(mixed, wo_ref[...], preferred_element_type=F32)
    r = alpha * x_ref[...] + sub
    mu = jnp.mean(r, axis=-1, keepdims=True)
    var = jnp.mean(jnp.square(r - mu), axis=-1, keepdims=True)
    y_ref[...] = (r - mu) * lax.rsqrt(var + LN_EPS) * g_ref[...] + b_ref[...]


def _out_mix(x2d, hm, mo, mz, ha, az, gain, wo, ln_g, ln_b, alpha, tm):
    n_rows = x2d.shape[0]
    row = lambda i: (i, 0)
    return pl.pallas_call(
        functools.partial(_out_kernel, alpha=alpha),
        grid=(n_rows // tm,),
        in_specs=[pl.BlockSpec((tm, D_MODEL), row)] + [pl.BlockSpec((tm, 512), row)] * 5
        + [_const_spec((1, M_WIDTH)), _const_spec((MIX_WIDTH, D_MODEL)),
           _const_spec((1, D_MODEL)), _const_spec((1, D_MODEL))],
        out_specs=pl.BlockSpec((tm, D_MODEL), row),
        out_shape=jax.ShapeDtypeStruct((n_rows, D_MODEL), F32),
        compiler_params=pltpu.CompilerParams(
            dimension_semantics=("arbitrary",), vmem_limit_bytes=VMEM_LIMIT),
        name="out_mix",
    )(x2d, hm, mo, mz, ha, az, gain[None, :], wo.astype(BF16), ln_g[None, :], ln_b[None, :])


def _mlstm_step_kernel(qc_ref, kc_ref, qr_ref, kr_ref, vr_ref, i_ref, f_ref, c0_ref, n0_ref, m0_ref,
                       h_ref, c_ref, n_ref, m_ref):
    lf = _log_sigmoid(f_ref[...])
    m0 = m0_ref[...]
    it = i_ref[...]
    m_new = jnp.maximum(lf + m0, it)
    fg = jnp.exp(lf + m0 - m_new)
    ig = jnp.exp(it - m_new)
    c_new = fg * c0_ref[...] + (ig * kc_ref[...]) * vr_ref[...]
    n_new = fg * n0_ref[...] + ig * kr_ref[...]
    num = jnp.sum(qc_ref[...] * c_new, axis=2, keepdims=True)
    den = jnp.sum(qr_ref[...] * n_new, axis=3, keepdims=True)
    h_ref[...] = num / jnp.maximum(jnp.abs(den), jnp.exp(-m_new))
    c_ref[...] = c_new
    n_ref[...] = n_new
    m_ref[...] = m_new


def _mlstm_step(mq, mk, mv, sm, c0, n0, m0, tb):
    db = mq.shape[0]
    col = lambda t: t.reshape(db, M_HEADS, M_DK, 1)
    rowv = lambda t: t.reshape(db, M_HEADS, 1, M_DK)
    sc = lambda t: t.reshape(db, M_HEADS, 1, 1)
    idx = lambda i: (i, 0, 0, 0)
    s_col = pl.BlockSpec((tb, M_HEADS, M_DK, 1), idx)
    s_row = pl.BlockSpec((tb, M_HEADS, 1, M_DK), idx)
    s_sc = pl.BlockSpec((tb, M_HEADS, 1, 1), idx)
    s_mat = pl.BlockSpec((tb, M_HEADS, M_DK, M_DV), idx)
    return pl.pallas_call(
        _mlstm_step_kernel,
        grid=(db // tb,),
        in_specs=[s_col, s_col, s_row, s_row, s_row, s_sc, s_sc, s_mat, s_row, s_sc],
        out_specs=[s_row, s_mat, s_row, s_sc],
        out_shape=(jax.ShapeDtypeStruct((db, M_HEADS, 1, M_DV), F32),
                   jax.ShapeDtypeStruct((db, M_HEADS, M_DK, M_DV), F32),
                   jax.ShapeDtypeStruct((db, M_HEADS, 1, M_DK), F32),
                   jax.ShapeDtypeStruct((db, M_HEADS, 1, 1), F32)),
        compiler_params=pltpu.CompilerParams(
            dimension_semantics=("arbitrary",), vmem_limit_bytes=VMEM_LIMIT),
        name="mlstm_step",
    )(col(mq), col(mk), rowv(mq), rowv(mk), rowv(mv),
      sc(sm[:, _SM_I:_SM_I + M_HEADS]), sc(sm[:, _SM_F:_SM_F + M_HEADS]),
      c0, rowv(n0), sc(m0))


SCORE_ROWS_PAD = 8


def _samp_index_kernel(pt_ref, q_ref, w_ref, knew_ref, kidx_hbm, out_ref, buf, sem, *, n_pages):
    b = pl.program_id(0)
    nb = pl.num_programs(0)
    slot = b % 2

    def copies(seq, sl):
        return [pltpu.make_async_copy(kidx_hbm.at[pt_ref[seq, p]], buf.at[sl, p], sem.at[sl])
                for p in range(n_pages)]

    @pl.when(b == 0)
    def _():
        for cp in copies(0, 0):
            cp.start()

    @pl.when(b + 1 < nb)
    def _():
        for cp in copies(b + 1, 1 - slot):
            cp.start()

    for cp in copies(b, slot):
        cp.wait()

    q = q_ref[0]
    wcol = w_ref[0] * (IDX_HEADS ** -0.5) * (IDX_DIM ** -0.5)

    def group(g, carry):
        rows = []
        for j in range(8):
            kt = buf[slot, g * 8 + j].astype(BF16)
            d = jnp.dot(q, kt, preferred_element_type=F32)
            rows.append(jnp.sum(jnp.maximum(d, 0.0) * wcol, axis=0, keepdims=True))
        out_ref[0, pl.ds(pl.multiple_of(g * 8, 8), 8), :] = jnp.concatenate(rows, axis=0)
        return carry

    lax.fori_loop(0, n_pages // 8, group, 0)
    kn = knew_ref[0].astype(BF16).astype(F32)
    dn = jnp.sum(q.astype(F32) * kn, axis=-1, keepdims=True)
    scn = jnp.sum(jnp.maximum(dn, 0.0) * wcol, axis=0, keepdims=True)
    lane = lax.broadcasted_iota(I32, (SCORE_ROWS_PAD, LANES), 1)
    row = lax.broadcasted_iota(I32, (SCORE_ROWS_PAD, LANES), 0)
    out_ref[0, n_pages:n_pages + SCORE_ROWS_PAD, :] = jnp.where((lane == 0) & (row == 0), scn, -jnp.inf)


def _samp_index(page_table, q, wcol, knew, kidx_t):
    db, n_pages = page_table.shape
    assert n_pages % 8 == 0
    rows = n_pages + SCORE_ROWS_PAD
    per_seq = lambda shape: pl.BlockSpec((1,) + shape, lambda b, pt: (b, 0, 0))
    grid_spec = pltpu.PrefetchScalarGridSpec(
        num_scalar_prefetch=1,
        grid=(db,),
        in_specs=[per_seq((IDX_HEADS, IDX_DIM)), per_seq((IDX_HEADS, 1)), per_seq((1, IDX_DIM)),
                  pl.BlockSpec(memory_space=pl.ANY)],
        out_specs=per_seq((rows, LANES)),
        scratch_shapes=[pltpu.VMEM((2, n_pages, IDX_DIM, PAGE_SIZE), F32),
                        pltpu.SemaphoreType.DMA((2,))],
    )
    return pl.pallas_call(
        functools.partial(_samp_index_kernel, n_pages=n_pages),
        grid_spec=grid_spec,
        out_shape=jax.ShapeDtypeStruct((db, rows, LANES), F32),
        compiler_params=pltpu.CompilerParams(
            dimension_semantics=("arbitrary",), vmem_limit_bytes=VMEM_LIMIT),
        name="samp_index",
    )(page_table, q, wcol, knew, kidx_t)


def _samp_topk_kernel(sc_ref, pt_ref, rid_ref, meta_ref, *, topk, idx_bits, n_pages):
    sc = sc_ref[...]
    tb, rows, _ = sc.shape
    kpos = (lax.broadcasted_iota(I32, sc.shape, 1) * PAGE_SIZE
            + lax.broadcasted_iota(I32, sc.shape, 2))

    def count(m):
        c = jnp.sum(jnp.where(m, 1.0, 0.0), axis=1, keepdims=True)
        return jnp.sum(c, axis=2, keepdims=True)

    thr = _kth_largest(lambda c: count(sc >= c), (tb, 1, 1), topk)
    need = topk - count(sc > thr)

    def idx_step(i, jcut):
        cand = jcut + lax.shift_left(jnp.int32(1), idx_bits - 1 - i)
        return jnp.where(count((sc == thr) & (kpos < cand)) < need, cand, jcut)

    jcut = lax.fori_loop(0, idx_bits, idx_step, jnp.zeros((tb, 1, 1), I32))
    jcut = jnp.where(thr > -jnp.inf, jcut, -1)
    sel = jnp.where((sc > thr) | ((sc == thr) & (kpos <= jcut)), 1.0, 0.0)

    slot = lax.broadcasted_iota(I32, (topk, LANES), 0).astype(F32)
    lane = lax.broadcasted_iota(I32, (topk, LANES), 1)
    lane1 = lax.broadcasted_iota(I32, (1, LANES), 1)
    ones8 = jnp.ones((8, LANES), BF16)
    zpad = jnp.zeros((LANES - rows, LANES), F32)
    for i in range(tb):
        sel_i = jnp.concatenate([sel[i], zpad], axis=0)
        incl = _cumsum_lanes(sel_i)
        tot = lax.dot_general(ones8, sel_i.astype(BF16), (((1,), (1,)), ((), ())),
                              preferred_element_type=F32)[0:1, :]
        end = _cumsum_lanes(tot)
        start = end - tot
        rj = jnp.sum(jnp.where(end <= slot, 1.0, 0.0), axis=-1, keepdims=True)
        onehot = jnp.where(lane.astype(F32) == rj, 1.0, 0.0)
        local = slot[:, :1] - jnp.sum(onehot * start, axis=-1, keepdims=True)
        incl_j = jnp.dot(onehot.astype(BF16), incl.astype(BF16), preferred_element_type=F32)
        off = jnp.sum(jnp.where(incl_j <= local, 1.0, 0.0), axis=-1, keepdims=True)
        phys = jnp.sum(onehot * pt_ref[i], axis=-1, keepdims=True)
        rid_ref[i] = (phys * PAGE_SIZE + off).astype(I32)
        n_past = end[:, n_pages - 1:n_pages]
        new_sel = tot[:, n_pages:n_pages + 1]
        meta_ref[i] = jnp.where(lane1 == 0, n_past, jnp.where(lane1 == 1, new_sel, 0.0)).astype(I32)


def _samp_topk(sc, ptf, n_keys, n_pages, tb):
    db, rows, _ = sc.shape
    topk = min(TOPK_MAX, n_keys // 4)
    blk = lambda shape: pl.BlockSpec((tb,) + shape, lambda i: (i, 0, 0))
    return pl.pallas_call(
        functools.partial(_samp_topk_kernel, topk=topk, idx_bits=(n_keys - 1).bit_length(),
                          n_pages=n_pages),
        grid=(db // tb,),
        in_specs=[blk((rows, LANES)), blk((1, LANES))],
        out_specs=[blk((topk, 1)), blk((1, LANES))],
        out_shape=(jax.ShapeDtypeStruct((db, topk, 1), I32), jax.ShapeDtypeStruct((db, 1, LANES), I32)),
        compiler_params=pltpu.CompilerParams(
            dimension_semantics=("arbitrary",), vmem_limit_bytes=VMEM_LIMIT),
        name="samp_topk",
    )(sc, ptf)


PAGE_SHIFT = PAGE_SIZE.bit_length() - 1


def _samp_attn_kernel(rid_ref, meta_ref, q_ref, knew_ref, vnew_ref, k_hbm, v_hbm, out_ref,
                      kbuf, vbuf, ksem, vsem, *, layer, nsel):
    b = pl.program_id(0)
    nb = pl.num_programs(0)
    slot = b % 2

    def row_copies(page, off, sl, j):
        return (pltpu.make_async_copy(k_hbm.at[layer, page, off], kbuf.at[sl, j], ksem.at[sl]),
                pltpu.make_async_copy(v_hbm.at[layer, page, off], vbuf.at[sl, j], vsem.at[sl]))

    def issue(seq, sl):
        def body(j, carry):
            rid = rid_ref[seq, j]
            page = lax.shift_right_logical(rid, jnp.int32(PAGE_SHIFT))
            for cp in row_copies(page, rid & (PAGE_SIZE - 1), sl, j):
                cp.start()
            return carry
        lax.fori_loop(0, nsel, body, 0, unroll=8)

    @pl.when(b == 0)
    def _():
        issue(0, 0)

    @pl.when(b + 1 < nb)
    def _():
        issue(b + 1, 1 - slot)

    def wait_row(j, carry):
        for cp in row_copies(0, 0, slot, j):
            cp.wait()
        return carry
    lax.fori_loop(0, nsel, wait_row, 0, unroll=8)

    n_past = meta_ref[b, 0]
    has_new = meta_ref[b, 1] > 0
    q = q_ref[...]
    s = jnp.sum(kbuf[slot] * q, axis=-1, keepdims=True)
    valid = lax.broadcasted_iota(I32, s.shape, 0) < n_past
    s = jnp.where(valid, s, NEG_BIG)
    s_new = jnp.sum(knew_ref[...] * q, axis=-1, keepdims=True)
    m = jnp.max(s, axis=0, keepdims=True)
    m = jnp.where(has_new, jnp.maximum(m, s_new), m)
    p = jnp.where(valid, jnp.exp(s - m), 0.0)
    p_new = jnp.where(has_new, jnp.exp(s_new - m), 0.0)
    l = jnp.sum(p, axis=0, keepdims=True) + p_new
    o = jnp.sum(p * vbuf[slot], axis=0, keepdims=True) + p_new * vnew_ref[...]
    out_ref[...] = o / l


def _samp_attn(rid, meta, q, knew, vnew, k_cache, v_cache, layer):
    db, nsel = rid.shape
    per_seq = pl.BlockSpec((1, A_HEADS, A_DH), lambda b, r, m: (b, 0, 0))
    grid_spec = pltpu.PrefetchScalarGridSpec(
        num_scalar_prefetch=2,
        grid=(db,),
        in_specs=[per_seq, per_seq, per_seq,
                  pl.BlockSpec(memory_space=pl.ANY), pl.BlockSpec(memory_space=pl.ANY)],
        out_specs=per_seq,
        scratch_shapes=[pltpu.VMEM((2, nsel, A_HEADS, A_DH), F32),
                        pltpu.VMEM((2, nsel, A_HEADS, A_DH), F32),
                        pltpu.SemaphoreType.DMA((2,)), pltpu.SemaphoreType.DMA((2,))],
    )
    return pl.pallas_call(
        functools.partial(_samp_attn_kernel, layer=layer, nsel=nsel),
        grid_spec=grid_spec,
        out_shape=jax.ShapeDtypeStruct((db, A_HEADS, A_DH), F32),
        compiler_params=pltpu.CompilerParams(
            dimension_semantics=("arbitrary",), vmem_limit_bytes=VMEM_LIMIT),
        name="samp_attn",
    )(rid, meta, q, knew, vnew, k_cache, v_cache)


def kernel(x_prompt, x_sample, state_C, state_n, state_m, cache_k, cache_v, cache_kidx, page_table,
           w_in, b_in, mh_gain, w_out, ln_g, ln_b):
    depth = w_in.shape[0]
    n_batch, seq, _ = x_prompt.shape
    db, dec_seq, _ = x_sample.shape
    assert dec_seq == 1, "one new token per sampled sequence"
    n_pages = page_table.shape[1]
    past = n_pages * PAGE_SIZE
    alpha = (2.0 * depth) ** 0.25
    pos_p = jnp.arange(seq)
    pos_s = past + jnp.arange(dec_seq)
    tm = min(512, seq)
    chunk = min(256, seq)
    tq, tk = min(256, seq), min(256, seq)

    hp = x_prompt.reshape(n_batch * seq, D_MODEL)
    hs = x_sample.reshape(db * dec_seq, D_MODEL)
    outs = {k: [] for k in ("kp", "vp", "ip", "Cp", "np", "mp", "ks", "vs", "is", "Cs", "ns", "ms")}
    for l in range(depth):
        weights = _prep_weights(w_in[l], b_in[l])
        (mq, mk, mv, mo, mz, aq, ak, av, az, iq, ik, sm, gt, akb, avb, ikb) = _project(
            hp, pos_p, weights, n_batch, tm, BF16)
        h_m, c_p, n_p, m_p = _mlstm_prompt(mq, mk, mv, sm, gt, n_batch, seq, chunk)
        h_a = _dsa_prompt(iq, sm, aq, ikb, akb, avb, n_batch, seq, tq, tk)
        hp_new = _out_mix(hp, h_m, mo, mz, h_a, az, mh_gain[l], w_out[l], ln_g[l], ln_b[l], alpha, tm)
        outs["kp"].append(ak.reshape(n_batch, seq, A_HEADS, A_DH))
        outs["vp"].append(av.reshape(n_batch, seq, A_HEADS, A_DH))
        outs["ip"].append(ik.reshape(n_batch, seq, IDX_DIM))
        outs["Cp"].append(c_p)
        outs["np"].append(n_p.reshape(n_batch, M_HEADS, M_DK))
        outs["mp"].append(m_p[:, :, 0, 0])
        hp = hp_new
        (smq, smk, smv, smo, smz, saq, sak, sav, saz, siq, sik, ssm, _, _, _, _) = _project(
            hs, jnp.tile(pos_s, db), weights, 1, min(128, db * dec_seq), F32)
        g_m, c_s, n_s, m_s = _mlstm_step(smq, smk, smv, ssm, state_C[l], state_n[l], state_m[l],
                                         tb=min(8, db))
        kidx_t = jnp.swapaxes(cache_kidx[l], 1, 2)
        sc = _samp_index(page_table, siq.reshape(db, IDX_HEADS, IDX_DIM),
                         ssm[:, _SM_W:_SM_W + IDX_HEADS, None], sik[:, None, :], kidx_t)
        ptf = jnp.pad(page_table.astype(F32), ((0, 0), (0, LANES - n_pages)))[:, None, :]
        rid, meta = _samp_topk(sc, ptf, past + dec_seq, n_pages, tb=min(8, db))
        heads = lambda t: t.astype(F32).reshape(db, A_HEADS, A_DH)
        g_a = _samp_attn(rid[:, :, 0], meta[:, 0, :2], heads(saq), heads(sak), heads(sav),
                         cache_k, cache_v, l)
        hs_new = _out_mix(hs, g_m.reshape(db, M_WIDTH), smo, smz, g_a.reshape(db, A_WIDTH), saz,
                          mh_gain[l], w_out[l], ln_g[l], ln_b[l], alpha, min(128, db))
        outs["ks"].append(sak.reshape(db, dec_seq, A_HEADS, A_DH))
        outs["vs"].append(sav.reshape(db, dec_seq, A_HEADS, A_DH))
        outs["is"].append(sik.reshape(db, dec_seq, IDX_DIM))
        outs["Cs"].append(c_s)
        outs["ns"].append(n_s.reshape(db, M_HEADS, M_DK))
        outs["ms"].append(m_s.reshape(db, M_HEADS))
        hs = hs_new
    st = lambda k: jnp.stack(outs[k])
    return (hp.reshape(n_batch, seq, D_MODEL), hs.reshape(db, dec_seq, D_MODEL),
            st("kp"), st("vp"), st("ip"), st("Cp"), st("np"), st("mp"),
            st("ks"), st("vs"), st("is"), st("Cs"), st("ns"), st("ms"))
```

```python
import functools

import jax
import jax.numpy as jnp
from jax import lax
from jax.experimental import pallas as pl
from jax.experimental.pallas import tpu as pltpu

F32 = jnp.float32
BF16 = jnp.bfloat16
I32 = jnp.int32

D_MODEL = 1024
PAGE_SIZE = 128
M_HEADS = 4
M_DK = 128
M_DV = 128
M_WIDTH = M_HEADS * M_DV
A_HEADS = 4
A_DH = 128
A_WIDTH = A_HEADS * A_DH
ROT_DIM = A_DH // 4
ROPE_THETA = 500000.0
IDX_HEADS = 8
IDX_DIM = 64
IDX_ROT = IDX_DIM // 4
TOPK_MAX = 256
MIX_WIDTH = M_WIDTH + A_WIDTH
LN_EPS = 1e-5

LANES = 128
NEG_BIG = -1e30
INT_MIN = -(2 ** 31)

_SPLITS = (
    ("m_q", M_WIDTH), ("m_k", M_WIDTH), ("m_v", M_WIDTH), ("m_o", M_WIDTH), ("m_z", M_WIDTH),
    ("m_i", M_HEADS), ("m_f", M_HEADS),
    ("a_q", A_WIDTH), ("a_k", A_WIDTH), ("a_v", A_WIDTH), ("a_z", A_WIDTH),
    ("i_q", IDX_HEADS * IDX_DIM), ("i_k", IDX_DIM), ("i_w", IDX_HEADS),
)
_BIG = ("m_q", "m_k", "m_v", "m_o", "m_z", "a_q", "a_k", "a_v", "a_z", "i_q")
_SM_I = IDX_DIM
_SM_F = IDX_DIM + M_HEADS
_SM_W = IDX_DIM + 2 * M_HEADS
_SM_QN = _SM_W + IDX_HEADS
_SM_KN = _SM_QN + A_HEADS
SCORE_BOUND_SQ = 1400.0
COUNT_ROWS = 128
VMEM_LIMIT = 56 * 1024 * 1024


def _offsets():
    offs, start = {}, 0
    for name, n in _SPLITS:
        offs[name] = (start, start + n)
        start += n
    return offs


def _log_sigmoid(x):
    return jnp.minimum(x, 0.0) - jnp.log1p(jnp.exp(-jnp.abs(x)))


def _sigmoid(x):
    return 1.0 / (1.0 + jnp.exp(-x))


def _rope_tables(pos, rot_dim, period):
    half = rot_dim // 2
    inv = ROPE_THETA ** (-jnp.arange(half, dtype=F32) / half)
    ang = pos.astype(F32)[:, None] * inv
    cos, sin = jnp.cos(ang), jnp.sin(ang)
    n = pos.shape[0]
    c = jnp.concatenate([cos, cos, jnp.ones((n, period - rot_dim), F32)], axis=-1)
    s = jnp.concatenate([sin, sin, jnp.zeros((n, period - rot_dim), F32)], axis=-1)
    reps = LANES // period
    return jnp.tile(c, (1, reps)), jnp.tile(s, (1, reps))


def _rope128(x, c, s, half, period):
    lane = lax.broadcasted_iota(I32, x.shape, 1) % period
    up = pltpu.roll(x, LANES - half, axis=1)
    dn = pltpu.roll(x, half, axis=1)
    t = jnp.where(lane < half, -up, dn)
    return x * c + t * s


def _proj_kernel(x_ref, wb_ref, ws_ref, wg_ref, bb_ref, bs_ref, bg_ref,
                 ch_ref, sh_ref, ci_ref, si_ref,
                 mq_ref, mk_ref, mv_ref, mo_ref, mz_ref, aq_ref, ak_ref, av_ref, az_ref,
                 iq_ref, ik_ref, sm_ref, gt_ref, akb_ref, avb_ref, ikb_ref):
    x = x_ref[...].astype(BF16)

    def piece(j):
        w = wb_ref[:, j * 512:(j + 1) * 512]
        return jnp.dot(x, w, preferred_element_type=F32) + bb_ref[:, j * 512:(j + 1) * 512]

    def rope512(p, c, s, half, period):
        return jnp.concatenate(
            [_rope128(p[:, g * LANES:(g + 1) * LANES], c, s, half, period) for g in range(4)], axis=-1)

    ch, sh = ch_ref[...], sh_ref[...]
    ci, si = ci_ref[...], si_ref[...]
    mq_ref[...] = piece(0).astype(mq_ref.dtype)
    mk_ref[...] = (piece(1) * (M_DK ** -0.5)).astype(mk_ref.dtype)
    mv_ref[...] = piece(2).astype(mv_ref.dtype)
    mo_ref[...] = piece(3)
    mz_ref[...] = piece(4)
    aq = rope512(piece(5), ch, sh, ROT_DIM // 2, LANES)
    aq_ref[...] = (aq * (A_DH ** -0.5)).astype(BF16)
    ak = rope512(piece(6), ch, sh, ROT_DIM // 2, LANES)
    ak_ref[...] = ak
    akb_ref[...] = ak.astype(BF16)
    av = piece(7)
    av_ref[...] = av
    avb_ref[...] = av.astype(BF16)
    az_ref[...] = piece(8)
    iq_ref[...] = rope512(piece(9), ci, si, IDX_ROT // 2, IDX_DIM).astype(BF16)
    psm = jnp.dot(x, ws_ref[...], preferred_element_type=F32) + bs_ref[...]
    lane = lax.broadcasted_iota(I32, psm.shape, 1)
    sm = psm
    for base, src_ref in ((_SM_QN, aq_ref), (_SM_KN, akb_ref)):
        for h in range(A_HEADS):
            v = src_ref[:, h * A_DH:(h + 1) * A_DH].astype(F32)
            sm = jnp.where(lane == base + h, jnp.sum(v * v, axis=-1, keepdims=True), sm)
    sm_ref[...] = sm
    ik = _rope128(psm, ci, si, IDX_ROT // 2, IDX_DIM)[:, :IDX_DIM]
    ik_ref[...] = ik
    ikb_ref[...] = ik.astype(BF16)
    gt = lax.dot_general(wg_ref[...], x, (((1,), (1,)), ((), ())), preferred_element_type=F32)
    gt_ref[...] = gt + bg_ref[...]


def _const_spec(shape):
    nd = len(shape)
    return pl.BlockSpec(shape, lambda *_: (0,) * nd, pipeline_mode=pl.Buffered(1))


def _project(x2d, pos, weights, n_batch, tm, qkv_dtype):
    wb, ws, wg, bb, bs, bg = weights
    n_rows = x2d.shape[0]
    n_pos = pos.shape[0]
    npb = n_pos // tm
    ch, sh = _rope_tables(pos, ROT_DIM, LANES)
    ci, si = _rope_tables(pos, IDX_ROT, IDX_DIM)

    row = lambda p, b: (b * npb + p, 0)
    tab = lambda p, b: (p, 0)
    wide = lambda dt: jax.ShapeDtypeStruct((n_rows, 512), dt)
    out_shape = (
        wide(qkv_dtype), wide(qkv_dtype), wide(qkv_dtype), wide(F32), wide(F32),
        wide(BF16), wide(F32), wide(F32), wide(F32),
        wide(BF16),
        jax.ShapeDtypeStruct((n_rows, IDX_DIM), F32),
        jax.ShapeDtypeStruct((n_rows, LANES), F32),
        jax.ShapeDtypeStruct((8, n_rows), F32),
        wide(BF16), wide(BF16),
        jax.ShapeDtypeStruct((n_rows, IDX_DIM), BF16),
    )
    out_specs = (
        [pl.BlockSpec((tm, 512), row)] * 10
        + [pl.BlockSpec((tm, IDX_DIM), row), pl.BlockSpec((tm, LANES), row),
           pl.BlockSpec((8, tm), lambda p, b: (0, b * npb + p)),
           pl.BlockSpec((tm, 512), row), pl.BlockSpec((tm, 512), row),
           pl.BlockSpec((tm, IDX_DIM), row)]
    )
    in_specs = [
        pl.BlockSpec((tm, D_MODEL), row),
        _const_spec(wb.shape), _const_spec(ws.shape), _const_spec(wg.shape),
        _const_spec(bb.shape), _const_spec(bs.shape), _const_spec(bg.shape),
        pl.BlockSpec((tm, LANES), tab), pl.BlockSpec((tm, LANES), tab),
        pl.BlockSpec((tm, LANES), tab), pl.BlockSpec((tm, LANES), tab),
    ]
    return pl.pallas_call(
        _proj_kernel,
        grid=(npb, n_batch),
        in_specs=in_specs,
        out_specs=out_specs,
        out_shape=out_shape,
        compiler_params=pltpu.CompilerParams(
            dimension_semantics=("arbitrary", "arbitrary"), vmem_limit_bytes=VMEM_LIMIT),
        name="proj_rope",
    )(x2d, wb, ws, wg, bb, bs, bg, ch, sh, ci, si)


def _prep_weights(w_in, b_in):
    offs = _offsets()
    col = lambda nm: w_in[:, offs[nm][0]:offs[nm][1]]
    bcol = lambda nm: b_in[offs[nm][0]:offs[nm][1]]
    wb = jnp.concatenate([col(nm) for nm in _BIG], axis=1).astype(BF16)
    bb = jnp.concatenate([bcol(nm) for nm in _BIG])[None, :]
    pad = LANES - (IDX_DIM + 2 * M_HEADS + IDX_HEADS)
    ws = jnp.concatenate([col("i_k"), col("m_i"), col("m_f"), col("i_w"),
                          jnp.zeros((D_MODEL, pad), w_in.dtype)], axis=1).astype(BF16)
    bs = jnp.concatenate([bcol("i_k"), bcol("m_i"), bcol("m_f"), bcol("i_w"),
                          jnp.zeros((pad,), b_in.dtype)])[None, :]
    wg = jnp.concatenate([col("m_i"), col("m_f")], axis=1).T.astype(BF16)
    bg = jnp.concatenate([bcol("m_i"), bcol("m_f")])[:, None]
    return wb, ws, wg, bb, bs, bg


def _cumsum_rows(x):
    n = x.shape[0]
    row = lax.broadcasted_iota(I32, x.shape, 0)
    s = 1
    while s < n:
        x = x + jnp.where(row >= s, pltpu.roll(x, s, axis=0), 0.0)
        s *= 2
    return x


def _cumsum_lanes(x):
    n = x.shape[1]
    lane = lax.broadcasted_iota(I32, x.shape, 1)
    s = 1
    while s < n:
        x = x + jnp.where(lane >= s, pltpu.roll(x, s, axis=1), 0.0)
        s *= 2
    return x


def _mlstm_kernel(mq_ref, mk_ref, mv_ref, sm_ref, gt_ref, h_ref, c_ref, n_ref, m_ref):
    c_idx = pl.program_id(1)
    L = mq_ref.shape[0]

    @pl.when(c_idx == 0)
    def _():
        c_ref[...] = jnp.zeros_like(c_ref)
        n_ref[...] = jnp.zeros_like(n_ref)
        m_ref[...] = jnp.zeros_like(m_ref)

    sm = sm_ref[...]
    gt = gt_ref[...]
    bcol = _cumsum_rows(_log_sigmoid(sm))
    brow = _cumsum_lanes(_log_sigmoid(gt))
    tpos = lax.broadcasted_iota(I32, (L, L), 0)
    spos = lax.broadcasted_iota(I32, (L, L), 1)
    causal = tpos >= spos

    for h in range(M_HEADS):
        hs = slice(h * M_DK, (h + 1) * M_DK)
        q = mq_ref[:, hs]
        k = mk_ref[:, hs]
        v = mv_ref[:, hs]
        li_c = sm[:, _SM_I + h:_SM_I + h + 1]
        b_c = bcol[:, _SM_F + h:_SM_F + h + 1]
        li_r = gt[h:h + 1, :]
        b_r = brow[M_HEADS + h:M_HEADS + h + 1, :]
        m_prev = m_ref[0, h][:, :1]
        c_prev = c_ref[0, h]
        n_prev = n_ref[0, h]

        dmat = jnp.where(causal, b_c + (li_r - b_r), -jnp.inf)
        inter = b_c + m_prev
        m_row = jnp.maximum(inter, jnp.max(dmat, axis=-1, keepdims=True))
        w_intra = jnp.exp(dmat - m_row)
        w_inter = jnp.exp(inter - m_row)
        s = lax.dot_general(q, k, (((1,), (1,)), ((), ())), preferred_element_type=F32)
        qk = s * w_intra
        num = (w_inter * jnp.dot(q, c_prev.astype(BF16), preferred_element_type=F32)
               + jnp.dot(qk.astype(BF16), v, preferred_element_type=F32))
        qf = q.astype(F32)
        den = (w_inter * jnp.sum(qf * n_prev, axis=-1, keepdims=True)
               + jnp.sum(qk, axis=-1, keepdims=True))
        h_ref[:, hs] = num / jnp.maximum(jnp.abs(den), jnp.exp(-m_row))

        b_last = b_c[L - 1:L, :]
        g_r = b_last - b_r + li_r
        m_new = jnp.maximum(b_last + m_prev, jnp.max(g_r, axis=-1, keepdims=True))
        decay = jnp.exp(b_last + m_prev - m_new)
        wk_c = jnp.exp(b_last - b_c + li_c - m_new)
        kf = k.astype(F32)
        wv = (wk_c * v.astype(F32)).astype(BF16)
        c_ref[0, h] = decay * c_prev + lax.dot_general(
            k, wv, (((0,), (0,)), ((), ())), preferred_element_type=F32)
        n_ref[0, h] = decay * n_prev + jnp.sum(wk_c * kf, axis=0, keepdims=True)
        m_ref[0, h] = jnp.broadcast_to(m_new, (1, LANES))


def _mlstm_prompt(mq, mk, mv, sm, gt, n_batch, seq, chunk):
    nc = seq // chunk
    row = lambda b, c: (b * nc + c, 0)
    st = lambda b, c: (b, 0, 0, 0)
    return pl.pallas_call(
        _mlstm_kernel,
        grid=(n_batch, nc),
        in_specs=[pl.BlockSpec((chunk, 512), row)] * 3
        + [pl.BlockSpec((chunk, LANES), row), pl.BlockSpec((8, chunk), lambda b, c: (0, b * nc + c))],
        out_specs=[pl.BlockSpec((chunk, 512), row),
                   pl.BlockSpec((1, M_HEADS, M_DK, M_DV), st),
                   pl.BlockSpec((1, M_HEADS, 1, M_DK), st),
                   pl.BlockSpec((1, M_HEADS, 1, LANES), st)],
        out_shape=(jax.ShapeDtypeStruct((n_batch * seq, 512), F32),
                   jax.ShapeDtypeStruct((n_batch, M_HEADS, M_DK, M_DV), F32),
                   jax.ShapeDtypeStruct((n_batch, M_HEADS, 1, M_DK), F32),
                   jax.ShapeDtypeStruct((n_batch, M_HEADS, 1, LANES), F32)),
        compiler_params=pltpu.CompilerParams(
            dimension_semantics=("arbitrary", "arbitrary"), vmem_limit_bytes=VMEM_LIMIT),
        name="mlstm_chunkwise",
    )(mq, mk, mv, sm, gt)


def _key_to_float(key):
    return pltpu.bitcast(key ^ ((key >> 31) & 0x7FFFFFFF), F32)


def _kth_largest(count_ge, shape, topk):
    def bit_step(i, key):
        cand = key + lax.shift_left(jnp.int32(1), 31 - i)
        return jnp.where(count_ge(_key_to_float(cand)) >= topk, cand, key)

    thr = _key_to_float(lax.fori_loop(0, 32, bit_step, jnp.full(shape, INT_MIN, I32)))
    return jnp.where(thr > -jnp.inf, thr, -jnp.inf)


def _dsa_prompt_kernel(iq_ref, sm_ref, aq_ref, kmax_ref, kidx_ref, k_ref, v_ref, out_ref,
                       sc_ref, wslab_ref, fslab_ref, islab_ref, m_ref, l_ref, acc_ref,
                       *, tq, tk, topk, idx_bits):
    qb = pl.program_id(1)
    ntile = ((qb + 1) * tq + tk - 1) // tk
    qpos = qb * tq + lax.broadcasted_iota(I32, (tq, 1), 0)
    lane_k = lax.broadcasted_iota(I32, (tq, tk), 1)
    wsc = sm_ref[:, _SM_W:_SM_W + IDX_HEADS] * (IDX_HEADS ** -0.5) * (IDX_DIM ** -0.5)
    q_idx = [iq_ref[:, h * IDX_DIM:(h + 1) * IDX_DIM] for h in range(IDX_HEADS)]

    ncol = tk // LANES
    cols = [slice(j * LANES, (j + 1) * LANES) for j in range(ncol)]
    for h in range(IDX_HEADS):
        wslab_ref[h] = jnp.broadcast_to(wsc[:, h:h + 1], (tq, LANES))

    def score_tile(t, carry):
        kt = kidx_ref[0, pl.ds(pl.multiple_of(t * tk, tk), tk), :]
        accs = [jnp.zeros((tq, LANES), F32) for _ in cols]
        for h in range(IDX_HEADS):
            d = lax.dot_general(q_idx[h], kt, (((1,), (1,)), ((), ())), preferred_element_type=F32)
            wh = wslab_ref[h]
            accs = [acc + wh * jnp.maximum(d[:, c], 0.0) for acc, c in zip(accs, cols)]
        kpos = t * tk + lane_k
        sc_ref[t] = jnp.where(kpos <= qpos, jnp.concatenate(accs, axis=1), -jnp.inf)
        return carry

    lax.fori_loop(0, ntile, score_tile, 0)

    @pl.when(ntile % 2 == 1)
    def _():
        sc_ref[ntile] = jnp.full((tq, tk), -jnp.inf, F32)

    rb = min(COUNT_ROWS, tq)
    lane128 = lax.broadcasted_iota(I32, (rb, LANES), 1)

    def set_f(x):
        fslab_ref[...] = jnp.broadcast_to(x, (tq, LANES))

    def set_i(x):
        islab_ref[...] = jnp.broadcast_to(x, (tq, LANES))

    def count(preds, use_pos=False):
        outs = [[] for _ in preds]
        for r0 in range(0, tq, rb):
            rows = slice(r0, r0 + rb)

            def body(p, accs, rows=rows):
                accs = list(accs)
                fv = fslab_ref[rows, :]
                iv = islab_ref[rows, :] if use_pos else None
                for t in (2 * p, 2 * p + 1):
                    for j, cs in enumerate(cols):
                        sc = sc_ref[t, rows, cs]
                        kpos = (t * tk + j * LANES) + lane128 if use_pos else None
                        for i, pred in enumerate(preds):
                            accs[i] = accs[i] + jnp.where(pred(sc, kpos, fv, iv), 1.0, 0.0)
                return tuple(accs)

            accs = lax.fori_loop(0, (ntile + 1) // 2, body,
                                 tuple(jnp.zeros((rb, LANES), F32) for _ in preds))
            for out, acc in zip(outs, accs):
                out.append(jnp.sum(acc, axis=-1, keepdims=True))
        return [jnp.concatenate(out, axis=0) for out in outs]

    def count_ge(c):
        set_f(c)
        return count([lambda sc, kp, fv, iv: sc >= fv])[0]

    thr = _kth_largest(count_ge, (tq, 1), topk)
    set_f(thr)
    n_gt, n_ge = count([lambda sc, kp, fv, iv: sc > fv, lambda sc, kp, fv, iv: sc >= fv])
    need = topk - n_gt
    cut_ties = jnp.max(jnp.where(thr > -jnp.inf, n_ge - topk, 1.0)) > 0.0

    lane_q = lax.broadcasted_iota(I32, (tq, LANES), 1)

    def bias_simple():
        def body(t, carry):
            fv = fslab_ref[...]
            for cs in cols:
                sc_ref[t, :, cs] = jnp.where(sc_ref[t, :, cs] >= fv, 0.0, NEG_BIG)
            return carry
        lax.fori_loop(0, ntile, body, 0)

    def bias_cut_ties():
        def idx_step(i, jcut):
            cand = jcut + lax.shift_left(jnp.int32(1), idx_bits - 1 - i)
            set_i(cand)
            cnt, = count([lambda sc, kp, fv, iv: (sc == fv) & (kp < iv)], use_pos=True)
            return jnp.where(cnt < need, cand, jcut)

        jcut = lax.fori_loop(0, idx_bits, idx_step, jnp.zeros((tq, 1), I32))
        set_i(jnp.where(thr > -jnp.inf, jcut, -1))

        def body(t, carry):
            fv = fslab_ref[...]
            iv = islab_ref[...]
            for j, cs in enumerate(cols):
                sc = sc_ref[t, :, cs]
                sel = (sc > fv) | ((sc == fv) & ((t * tk + j * LANES) + lane_q <= iv))
                sc_ref[t, :, cs] = jnp.where(sel, 0.0, NEG_BIG)
            return carry
        lax.fori_loop(0, ntile, body, 0)

    lax.cond(cut_ties, bias_cut_ties, bias_simple)

    def scores(t, h):
        hs = slice(h * A_DH, (h + 1) * A_DH)
        kh = k_ref[0, pl.ds(pl.multiple_of(t * tk, tk), tk), hs]
        s = lax.dot_general(aq_ref[:, hs], kh, (((1,), (1,)), ((), ())), preferred_element_type=F32)
        return s + sc_ref[t]

    l_ref[...] = jnp.zeros_like(l_ref)
    acc_ref[...] = jnp.zeros_like(acc_ref)

    def row_max():
        m_ref[...] = jnp.full_like(m_ref, NEG_BIG)

        def max_sweep(t, carry):
            for h in range(A_HEADS):
                s = scores(t, h)
                m = m_ref[h]
                for c in cols:
                    m = jnp.maximum(m, s[:, c])
                m_ref[h] = m
            return carry

        lax.fori_loop(0, ntile, max_sweep, 0)
        for h in range(A_HEADS):
            m_ref[h] = jnp.broadcast_to(jnp.max(m_ref[h], axis=-1, keepdims=True), (tq, LANES))

    def no_max():
        m_ref[...] = jnp.zeros_like(m_ref)

    bound_sq = sm_ref[:, _SM_QN:_SM_QN + A_HEADS] * kmax_ref[0]
    lax.cond(jnp.max(bound_sq) <= SCORE_BOUND_SQ, no_max, row_max)

    def sum_sweep(t, carry):
        start = pl.multiple_of(t * tk, tk)
        for h in range(A_HEADS):
            s = scores(t, h)
            mb = m_ref[h]
            ps = [jnp.exp(s[:, c] - mb) for c in cols]
            lsum = ps[0]
            for pj in ps[1:]:
                lsum = lsum + pj
            l_ref[h] = l_ref[h] + lsum
            vh = v_ref[0, pl.ds(start, tk), h * A_DH:(h + 1) * A_DH]
            p = jnp.concatenate(ps, axis=1).astype(BF16)
            acc_ref[h] = acc_ref[h] + jnp.dot(p, vh, preferred_element_type=F32)
        return carry

    lax.fori_loop(0, ntile, sum_sweep, 0)
    for h in range(A_HEADS):
        out_ref[:, h * A_DH:(h + 1) * A_DH] = acc_ref[h] / jnp.sum(l_ref[h], axis=-1, keepdims=True)


def _dsa_prompt(iq, sm, aq, ikb, akb, avb, n_batch, seq, tq, tk):
    nq = seq // tq
    topk = min(TOPK_MAX, seq // 4)
    row = lambda b, q: (b * nq + q, 0)
    full = lambda b, q: (b, 0, 0)
    kern = functools.partial(_dsa_prompt_kernel, tq=tq, tk=tk, topk=topk,
                             idx_bits=(seq - 1).bit_length())
    kmax = jnp.max(sm[:, _SM_KN:_SM_KN + A_HEADS].reshape(n_batch, seq, A_HEADS), axis=1, keepdims=True)
    return pl.pallas_call(
        kern,
        grid=(n_batch, nq),
        in_specs=[pl.BlockSpec((tq, 512), row), pl.BlockSpec((tq, LANES), row),
                  pl.BlockSpec((tq, 512), row), pl.BlockSpec((1, 1, A_HEADS), full),
                  pl.BlockSpec((1, seq, IDX_DIM), full, pipeline_mode=pl.Buffered(1)),
                  pl.BlockSpec((1, seq, 512), full, pipeline_mode=pl.Buffered(1)),
                  pl.BlockSpec((1, seq, 512), full, pipeline_mode=pl.Buffered(1))],
        out_specs=pl.BlockSpec((tq, 512), row),
        out_shape=jax.ShapeDtypeStruct((n_batch * seq, 512), F32),
        scratch_shapes=[pltpu.VMEM((seq // tk + 1, tq, tk), F32),
                        pltpu.VMEM((IDX_HEADS, tq, LANES), F32),
                        pltpu.VMEM((tq, LANES), F32), pltpu.VMEM((tq, LANES), I32),
                        pltpu.VMEM((A_HEADS, tq, LANES), F32), pltpu.VMEM((A_HEADS, tq, LANES), F32),
                        pltpu.VMEM((A_HEADS, tq, A_DH), F32)],
        compiler_params=pltpu.CompilerParams(
            dimension_semantics=("arbitrary", "arbitrary"), vmem_limit_bytes=VMEM_LIMIT),
        name="dsa_prompt",
    )(iq, sm, aq, kmax, ikb.reshape(n_batch, seq, IDX_DIM), akb.reshape(n_batch, seq, 512),
      avb.reshape(n_batch, seq, 512))


def _out_kernel(x_ref, hm_ref, mo_ref, mz_ref, ha_ref, az_ref, gain_ref, wo_ref, g_ref, b_ref,
                y_ref, *, alpha):
    hm = hm_ref[...] * _sigmoid(mo_ref[...])
    parts = []
    for h in range(M_HEADS):
        hh = hm[:, h * M_DV:(h + 1) * M_DV]
        mu = jnp.mean(hh, axis=-1, keepdims=True)
        var = jnp.mean(jnp.square(hh - mu), axis=-1, keepdims=True)
        parts.append((hh - mu) * lax.rsqrt(var + LN_EPS))
    mz = mz_ref[...]
    az = az_ref[...]
    g_m = jnp.concatenate(parts, axis=-1) * gain_ref[...] * (mz * _sigmoid(mz))
    g_a = ha_ref[...] * (az * _sigmoid(az))
    mixed = jnp.concatenate([g_m, g_a], axis=-1).astype(BF16)
    sub = jnp.dot(mixed, wo_ref[...], preferred_element_type=F32)
    r = alpha * x_ref[...] + sub
    mu = jnp.mean(r, axis=-1, keepdims=True)
    var = jnp.mean(jnp.square(r - mu), axis=-1, keepdims=True)
    y_ref[...] = (r - mu) * lax.rsqrt(var + LN_EPS) * g_ref[...] + b_ref[...]


def _out_mix(x2d, hm, mo, mz, ha, az, gain, wo, ln_g, ln_b, alpha, tm):
    n_rows = x2d.shape[0]
    row = lambda i: (i, 0)
    return pl.pallas_call(
        functools.partial(_out_kernel, alpha=alpha),
        grid=(n_rows // tm,),
        in_specs=[pl.BlockSpec((tm, D_MODEL), row)] + [pl.BlockSpec((tm, 512), row)] * 5
        + [_const_spec((1, M_WIDTH)), _const_spec((MIX_WIDTH, D_MODEL)),
           _const_spec((1, D_MODEL)), _const_spec((1, D_MODEL))],
        out_specs=pl.BlockSpec((tm, D_MODEL), row),
        out_shape=jax.ShapeDtypeStruct((n_rows, D_MODEL), F32),
        compiler_params=pltpu.CompilerParams(
            dimension_semantics=("arbitrary",), vmem_limit_bytes=VMEM_LIMIT),
        name="out_mix",
    )(x2d, hm, mo, mz, ha, az, gain[None, :], wo.astype(BF16), ln_g[None, :], ln_b[None, :])


def _mlstm_step_kernel(qc_ref, kc_ref, qr_ref, kr_ref, vr_ref, i_ref, f_ref, c0_ref, n0_ref, m0_ref,
                       h_ref, c_ref, n_ref, m_ref):
    lf = _log_sigmoid(f_ref[...])
    m0 = m0_ref[...]
    it = i_ref[...]
    m_new = jnp.maximum(lf + m0, it)
    fg = jnp.exp(lf + m0 - m_new)
    ig = jnp.exp(it - m_new)
    c_new = fg * c0_ref[...] + (ig * kc_ref[...]) * vr_ref[...]
    n_new = fg * n0_ref[...] + ig * kr_ref[...]
    num = jnp.sum(qc_ref[...] * c_new, axis=2, keepdims=True)
    den = jnp.sum(qr_ref[...] * n_new, axis=3, keepdims=True)
    h_ref[...] = num / jnp.maximum(jnp.abs(den), jnp.exp(-m_new))
    c_ref[...] = c_new
    n_ref[...] = n_new
    m_ref[...] = m_new


def _mlstm_step(mq, mk, mv, sm, c0, n0, m0, tb):
    db = mq.shape[0]
    col = lambda t: t.reshape(db, M_HEADS, M_DK, 1)
    rowv = lambda t: t.reshape(db, M_HEADS, 1, M_DK)
    sc = lambda t: t.reshape(db, M_HEADS, 1, 1)
    idx = lambda i: (i, 0, 0, 0)
    s_col = pl.BlockSpec((tb, M_HEADS, M_DK, 1), idx)
    s_row = pl.BlockSpec((tb, M_HEADS, 1, M_DK), idx)
    s_sc = pl.BlockSpec((tb, M_HEADS, 1, 1), idx)
    s_mat = pl.BlockSpec((tb, M_HEADS, M_DK, M_DV), idx)
    return pl.pallas_call(
        _mlstm_step_kernel,
        grid=(db // tb,),
        in_specs=[s_col, s_col, s_row, s_row, s_row, s_sc, s_sc, s_mat, s_row, s_sc],
        out_specs=[s_row, s_mat, s_row, s_sc],
        out_shape=(jax.ShapeDtypeStruct((db, M_HEADS, 1, M_DV), F32),
                   jax.ShapeDtypeStruct((db, M_HEADS, M_DK, M_DV), F32),
                   jax.ShapeDtypeStruct((db, M_HEADS, 1, M_DK), F32),
                   jax.ShapeDtypeStruct((db, M_HEADS, 1, 1), F32)),
        compiler_params=pltpu.CompilerParams(
            dimension_semantics=("arbitrary",), vmem_limit_bytes=VMEM_LIMIT),
        name="mlstm_step",
    )(col(mq), col(mk), rowv(mq), rowv(mk), rowv(mv),
      sc(sm[:, _SM_I:_SM_I + M_HEADS]), sc(sm[:, _SM_F:_SM_F + M_HEADS]),
      c0, rowv(n0), sc(m0))


SCORE_ROWS_PAD = 8


def _samp_index_kernel(pt_ref, q_ref, w_ref, knew_ref, kidx_hbm, out_ref, buf, sem, *, n_pages):
    b = pl.program_id(0)
    nb = pl.num_programs(0)
    slot = b % 2

    def copies(seq, sl):
        return [pltpu.make_async_copy(kidx_hbm.at[pt_ref[seq, p]], buf.at[sl, p], sem.at[sl])
                for p in range(n_pages)]

    @pl.when(b == 0)
    def _():
        for cp in copies(0, 0):
            cp.start()

    @pl.when(b + 1 < nb)
    def _():
        for cp in copies(b + 1, 1 - slot):
            cp.start()

    for cp in copies(b, slot):
        cp.wait()

    q = q_ref[0]
    wcol = w_ref[0] * (IDX_HEADS ** -0.5) * (IDX_DIM ** -0.5)

    def group(g, carry):
        rows = []
        for j in range(8):
            kt = buf[slot, g * 8 + j].astype(BF16)
            d = jnp.dot(q, kt, preferred_element_type=F32)
            rows.append(jnp.sum(jnp.maximum(d, 0.0) * wcol, axis=0, keepdims=True))
        out_ref[0, pl.ds(pl.multiple_of(g * 8, 8), 8), :] = jnp.concatenate(rows, axis=0)
        return carry

    lax.fori_loop(0, n_pages // 8, group, 0)
    kn = knew_ref[0].astype(BF16).astype(F32)
    dn = jnp.sum(q.astype(F32) * kn, axis=-1, keepdims=True)
    scn = jnp.sum(jnp.maximum(dn, 0.0) * wcol, axis=0, keepdims=True)
    lane = lax.broadcasted_iota(I32, (SCORE_ROWS_PAD, LANES), 1)
    row = lax.broadcasted_iota(I32, (SCORE_ROWS_PAD, LANES), 0)
    out_ref[0, n_pages:n_pages + SCORE_ROWS_PAD, :] = jnp.where((lane == 0) & (row == 0), scn, -jnp.inf)


def _samp_index(page_table, q, wcol, knew, kidx_t):
    db, n_pages = page_table.shape
    assert n_pages % 8 == 0
    rows = n_pages + SCORE_ROWS_PAD
    per_seq = lambda shape: pl.BlockSpec((1,) + shape, lambda b, pt: (b, 0, 0))
    grid_spec = pltpu.PrefetchScalarGridSpec(
        num_scalar_prefetch=1,
        grid=(db,),
        in_specs=[per_seq((IDX_HEADS, IDX_DIM)), per_seq((IDX_HEADS, 1)), per_seq((1, IDX_DIM)),
                  pl.BlockSpec(memory_space=pl.ANY)],
        out_specs=per_seq((rows, LANES)),
        scratch_shapes=[pltpu.VMEM((2, n_pages, IDX_DIM, PAGE_SIZE), F32),
                        pltpu.SemaphoreType.DMA((2,))],
    )
    return pl.pallas_call(
        functools.partial(_samp_index_kernel, n_pages=n_pages),
        grid_spec=grid_spec,
        out_shape=jax.ShapeDtypeStruct((db, rows, LANES), F32),
        compiler_params=pltpu.CompilerParams(
            dimension_semantics=("arbitrary",), vmem_limit_bytes=VMEM_LIMIT),
        name="samp_index",
    )(page_table, q, wcol, knew, kidx_t)


def _samp_topk_kernel(sc_ref, pt_ref, rid_ref, meta_ref, *, topk, idx_bits, n_pages):
    sc = sc_ref[...]
    tb, rows, _ = sc.shape
    kpos = (lax.broadcasted_iota(I32, sc.shape, 1) * PAGE_SIZE
            + lax.broadcasted_iota(I32, sc.shape, 2))

    def count(m):
        c = jnp.sum(jnp.where(m, 1.0, 0.0), axis=1, keepdims=True)
        return jnp.sum(c, axis=2, keepdims=True)

    thr = _kth_largest(lambda c: count(sc >= c), (tb, 1, 1), topk)
    need = topk - count(sc > thr)

    def idx_step(i, jcut):
        cand = jcut + lax.shift_left(jnp.int32(1), idx_bits - 1 - i)
        return jnp.where(count((sc == thr) & (kpos < cand)) < need, cand, jcut)

    jcut = lax.fori_loop(0, idx_bits, idx_step, jnp.zeros((tb, 1, 1), I32))
    jcut = jnp.where(thr > -jnp.inf, jcut, -1)
    sel = jnp.where((sc > thr) | ((sc == thr) & (kpos <= jcut)), 1.0, 0.0)

    slot = lax.broadcasted_iota(I32, (topk, LANES), 0).astype(F32)
    lane = lax.broadcasted_iota(I32, (topk, LANES), 1)
    lane1 = lax.broadcasted_iota(I32, (1, LANES), 1)
    ones8 = jnp.ones((8, LANES), BF16)
    zpad = jnp.zeros((LANES - rows, LANES), F32)
    for i in range(tb):
        sel_i = jnp.concatenate([sel[i], zpad], axis=0)
        incl = _cumsum_lanes(sel_i)
        tot = lax.dot_general(ones8, sel_i.astype(BF16), (((1,), (1,)), ((), ())),
                              preferred_element_type=F32)[0:1, :]
        end = _cumsum_lanes(tot)
        start = end - tot
        rj = jnp.sum(jnp.where(end <= slot, 1.0, 0.0), axis=-1, keepdims=True)
        onehot = jnp.where(lane.astype(F32) == rj, 1.0, 0.0)
        local = slot[:, :1] - jnp.sum(onehot * start, axis=-1, keepdims=True)
        incl_j = jnp.dot(onehot.astype(BF16), incl.astype(BF16), preferred_element_type=F32)
        off = jnp.sum(jnp.where(incl_j <= local, 1.0, 0.0), axis=-1, keepdims=True)
        phys = jnp.sum(onehot * pt_ref[i], axis=-1, keepdims=True)
        rid_ref[i] = (phys * PAGE_SIZE + off).astype(I32)
        n_past = end[:, n_pages - 1:n_pages]
        new_sel = tot[:, n_pages:n_pages + 1]
        meta_ref[i] = jnp.where(lane1 == 0, n_past, jnp.where(lane1 == 1, new_sel, 0.0)).astype(I32)


def _samp_topk(sc, ptf, n_keys, n_pages, tb):
    db, rows, _ = sc.shape
    topk = min(TOPK_MAX, n_keys // 4)
    blk = lambda shape: pl.BlockSpec((tb,) + shape, lambda i: (i, 0, 0))
    return pl.pallas_call(
        functools.partial(_samp_topk_kernel, topk=topk, idx_bits=(n_keys - 1).bit_length(),
                          n_pages=n_pages),
        grid=(db // tb,),
        in_specs=[blk((rows, LANES)), blk((1, LANES))],
        out_specs=[blk((topk, 1)), blk((1, LANES))],
        out_shape=(jax.ShapeDtypeStruct((db, topk, 1), I32), jax.ShapeDtypeStruct((db, 1, LANES), I32)),
        compiler_params=pltpu.CompilerParams(
            dimension_semantics=("arbitrary",), vmem_limit_bytes=VMEM_LIMIT),
        name="samp_topk",
    )(sc, ptf)


PAGE_SHIFT = PAGE_SIZE.bit_length() - 1


def _samp_attn_kernel(rid_ref, meta_ref, q_ref, knew_ref, vnew_ref, k_hbm, v_hbm, out_ref,
                      kbuf, vbuf, ksem, vsem, *, layer, nsel):
    b = pl.program_id(0)
    nb = pl.num_programs(0)
    slot = b % 2

    def row_copies(page, off, sl, j):
        return (pltpu.make_async_copy(k_hbm.at[layer, page, off], kbuf.at[sl, j], ksem.at[sl]),
                pltpu.make_async_copy(v_hbm.at[layer, page, off], vbuf.at[sl, j], vsem.at[sl]))

    def issue(seq, sl):
        def body(j, carry):
            rid = rid_ref[seq, j]
            page = lax.shift_right_logical(rid, jnp.int32(PAGE_SHIFT))
            for cp in row_copies(page, rid & (PAGE_SIZE - 1), sl, j):
                cp.start()
            return carry
        lax.fori_loop(0, nsel, body, 0, unroll=8)

    @pl.when(b == 0)
    def _():
        issue(0, 0)

    @pl.when(b + 1 < nb)
    def _():
        issue(b + 1, 1 - slot)

    def wait_row(j, carry):
        for cp in row_copies(0, 0, slot, j):
            cp.wait()
        return carry
    lax.fori_loop(0, nsel, wait_row, 0, unroll=8)

    n_past = meta_ref[b, 0]
    has_new = meta_ref[b, 1] > 0
    q = q_ref[...]
    s = jnp.sum(kbuf[slot] * q, axis=-1, keepdims=True)
    valid = lax.broadcasted_iota(I32, s.shape, 0) < n_past
    s = jnp.where(valid, s, NEG_BIG)
    s_new = jnp.sum(knew_ref[...] * q, axis=-1, keepdims=True)
    m = jnp.max(s, axis=0, keepdims=True)
    m = jnp.where(has_new, jnp.maximum(m, s_new), m)
    p = jnp.where(valid, jnp.exp(s - m), 0.0)
    p_new = jnp.where(has_new, jnp.exp(s_new - m), 0.0)
    l = jnp.sum(p, axis=0, keepdims=True) + p_new
    o = jnp.sum(p * vbuf[slot], axis=0, keepdims=True) + p_new * vnew_ref[...]
    out_ref[...] = o / l


def _samp_attn(rid, meta, q, knew, vnew, k_cache, v_cache, layer):
    db, nsel = rid.shape
    per_seq = pl.BlockSpec((1, A_HEADS, A_DH), lambda b, r, m: (b, 0, 0))
    grid_spec = pltpu.PrefetchScalarGridSpec(
        num_scalar_prefetch=2,
        grid=(db,),
        in_specs=[per_seq, per_seq, per_seq,
                  pl.BlockSpec(memory_space=pl.ANY), pl.BlockSpec(memory_space=pl.ANY)],
        out_specs=per_seq,
        scratch_shapes=[pltpu.VMEM((2, nsel, A_HEADS, A_DH), F32),
                        pltpu.VMEM((2, nsel, A_HEADS, A_DH), F32),
                        pltpu.SemaphoreType.DMA((2,)), pltpu.SemaphoreType.DMA((2,))],
    )
    return pl.pallas_call(
        functools.partial(_samp_attn_kernel, layer=layer, nsel=nsel),
        grid_spec=grid_spec,
        out_shape=jax.ShapeDtypeStruct((db, A_HEADS, A_DH), F32),
        compiler_params=pltpu.CompilerParams(
            dimension_semantics=("arbitrary",), vmem_limit_bytes=VMEM_LIMIT),
        name="samp_attn",
    )(rid, meta, q, knew, vnew, k_cache, v_cache)


def kernel(x_prompt, x_sample, state_C, state_n, state_m, cache_k, cache_v, cache_kidx, page_table,
           w_in, b_in, mh_gain, w_out, ln_g, ln_b):
    depth = w_in.shape[0]
    n_batch, seq, _ = x_prompt.shape
    db, dec_seq, _ = x_sample.shape
    assert dec_seq == 1, "one new token per sampled sequence"
    n_pages = page_table.shape[1]
    past = n_pages * PAGE_SIZE
    alpha = (2.0 * depth) ** 0.25
    pos_p = jnp.arange(seq)
    pos_s = past + jnp.arange(dec_seq)
    tm = min(512, seq)
    chunk = min(256, seq)
    tq, tk = min(256, seq), min(256, seq)

    hp = x_prompt.reshape(n_batch * seq, D_MODEL)
    hs = x_sample.reshape(db * dec_seq, D_MODEL)
    outs = {k: [] for k in ("kp", "vp", "ip", "Cp", "np", "mp", "ks", "vs", "is", "Cs", "ns", "ms")}
    for l in range(depth):
        weights = _prep_weights(w_in[l], b_in[l])
        (mq, mk, mv, mo, mz, aq, ak, av, az, iq, ik, sm, gt, akb, avb, ikb) = _project(
            hp, pos_p, weights, n_batch, tm, BF16)
        h_m, c_p, n_p, m_p = _mlstm_prompt(mq, mk, mv, sm, gt, n_batch, seq, chunk)
        h_a = _dsa_prompt(iq, sm, aq, ikb, akb, avb, n_batch, seq, tq, tk)
        hp_new = _out_mix(hp, h_m, mo, mz, h_a, az, mh_gain[l], w_out[l], ln_g[l], ln_b[l], alpha, tm)
        outs["kp"].append(ak.reshape(n_batch, seq, A_HEADS, A_DH))
        outs["vp"].append(av.reshape(n_batch, seq, A_HEADS, A_DH))
        outs["ip"].append(ik.reshape(n_batch, seq, IDX_DIM))
        outs["Cp"].append(c_p)
        outs["np"].append(n_p.reshape(n_batch, M_HEADS, M_DK))
        outs["mp"].append(m_p[:, :, 0, 0])
        hp = hp_new
        (smq, smk, smv, smo, smz, saq, sak, sav, saz, siq, sik, ssm, _, _, _, _) = _project(
            hs, jnp.tile(pos_s, db), weights, 1, min(128, db * dec_seq), F32)
        g_m, c_s, n_s, m_s = _mlstm_step(smq, smk, smv, ssm, state_C[l], state_n[l], state_m[l],
                                         tb=min(8, db))
        kidx_t = jnp.swapaxes(cache_kidx[l], 1, 2)
        sc = _samp_index(page_table, siq.reshape(db, IDX_HEADS, IDX_DIM),
                         ssm[:, _SM_W:_SM_W + IDX_HEADS, None], sik[:, None, :], kidx_t)
        ptf = jnp.pad(page_table.astype(F32), ((0, 0), (0, LANES - n_pages)))[:, None, :]
        rid, meta = _samp_topk(sc, ptf, past + dec_seq, n_pages, tb=min(8, db))
        heads = lambda t: t.astype(F32).reshape(db, A_HEADS, A_DH)
        g_a = _samp_attn(rid[:, :, 0], meta[:, 0, :2], heads(saq), heads(sak), heads(sav),
                         cache_k, cache_v, l)
        hs_new = _out_mix(hs, g_m.reshape(db, M_WIDTH), smo, smz, g_a.reshape(db, A_WIDTH), saz,
                          mh_gain[l], w_out[l], ln_g[l], ln_b[l], alpha, min(128, db))
        outs["ks"].append(sak.reshape(db, dec_seq, A_HEADS, A_DH))
        outs["vs"].append(sav.reshape(db, dec_seq, A_HEADS, A_DH))
        outs["is"].append(sik.reshape(db, dec_seq, IDX_DIM))
        outs["Cs"].append(c_s)
        outs["ns"].append(n_s.reshape(db, M_HEADS, M_DK))
        outs["ms"].append(m_s.reshape(db, M_HEADS))
        hs = hs_new
    st = lambda k: jnp.stack(outs[k])
    return (hp.reshape(n_batch, seq, D_MODEL), hs.reshape(db, dec_seq, D_MODEL),
            st("kp"), st("vp"), st("ip"), st("Cp"), st("np"), st("mp"),
            st("ks"), st("vs"), st("is"), st("Cs"), st("ns"), st("ms"))
```

```python
import functools

import jax
import jax.numpy as jnp
from jax import lax
from jax.experimental import pallas as pl
from jax.experimental.pallas import tpu as pltpu

F32 = jnp.float32
BF16 = jnp.bfloat16
I32 = jnp.int32

D_MODEL = 1024
PAGE_SIZE = 128
M_HEADS = 4
M_DK = 128
M_DV = 128
M_WIDTH = M_HEADS * M_DV
A_HEADS = 4
A_DH = 128
A_WIDTH = A_HEADS * A_DH
ROT_DIM = A_DH // 4
ROPE_THETA = 500000.0
IDX_HEADS = 8
IDX_DIM = 64
IDX_ROT = IDX_DIM // 4
TOPK_MAX = 256
MIX_WIDTH = M_WIDTH + A_WIDTH
LN_EPS = 1e-5

LANES = 128
NEG_BIG = -1e30
INT_MIN = -(2 ** 31)

_SPLITS = (
    ("m_q", M_WIDTH), ("m_k", M_WIDTH), ("m_v", M_WIDTH), ("m_o", M_WIDTH), ("m_z", M_WIDTH),
    ("m_i", M_HEADS), ("m_f", M_HEADS),
    ("a_q", A_WIDTH), ("a_k", A_WIDTH), ("a_v", A_WIDTH), ("a_z", A_WIDTH),
    ("i_q", IDX_HEADS * IDX_DIM), ("i_k", IDX_DIM), ("i_w", IDX_HEADS),
)
_BIG = ("m_q", "m_k", "m_v", "m_o", "m_z", "a_q", "a_k", "a_v", "a_z", "i_q")
_SM_I = IDX_DIM
_SM_F = IDX_DIM + M_HEADS
_SM_W = IDX_DIM + 2 * M_HEADS
_SM_QN = _SM_W + IDX_HEADS
_SM_KN = _SM_QN + A_HEADS
SCORE_BOUND_SQ = 1400.0
COUNT_ROWS = 128
VMEM_LIMIT = 56 * 1024 * 1024


def _offsets():
    offs, start = {}, 0
    for name, n in _SPLITS:
        offs[name] = (start, start + n)
        start += n
    return offs


def _log_sigmoid(x):
    return jnp.minimum(x, 0.0) - jnp.log1p(jnp.exp(-jnp.abs(x)))


def _sigmoid(x):
    return 1.0 / (1.0 + jnp.exp(-x))


def _rope_tables(pos, rot_dim, period):
    half = rot_dim // 2
    inv = ROPE_THETA ** (-jnp.arange(half, dtype=F32) / half)
    ang = pos.astype(F32)[:, None] * inv
    cos, sin = jnp.cos(ang), jnp.sin(ang)
    n = pos.shape[0]
    c = jnp.concatenate([cos, cos, jnp.ones((n, period - rot_dim), F32)], axis=-1)
    s = jnp.concatenate([sin, sin, jnp.zeros((n, period - rot_dim), F32)], axis=-1)
    reps = LANES // period
    return jnp.tile(c, (1, reps)), jnp.tile(s, (1, reps))


def _rope128(x, c, s, half, period):
    lane = lax.broadcasted_iota(I32, x.shape, 1) % period
    up = pltpu.roll(x, LANES - half, axis=1)
    dn = pltpu.roll(x, half, axis=1)
    t = jnp.where(lane < half, -up, dn)
    return x * c + t * s


def _proj_kernel(x_ref, wb_ref, ws_ref, wg_ref, bb_ref, bs_ref, bg_ref,
                 ch_ref, sh_ref, ci_ref, si_ref,
                 mq_ref, mk_ref, mv_ref, mo_ref, mz_ref, aq_ref, ak_ref, av_ref, az_ref,
                 iq_ref, ik_ref, sm_ref, gt_ref, akb_ref, avb_ref, ikb_ref):
    x = x_ref[...].astype(BF16)

    def piece(j):
        w = wb_ref[:, j * 512:(j + 1) * 512]
        return jnp.dot(x, w, preferred_element_type=F32) + bb_ref[:, j * 512:(j + 1) * 512]

    def rope512(p, c, s, half, period):
        return jnp.concatenate(
            [_rope128(p[:, g * LANES:(g + 1) * LANES], c, s, half, period) for g in range(4)], axis=-1)

    ch, sh = ch_ref[...], sh_ref[...]
    ci, si = ci_ref[...], si_ref[...]
    mq_ref[...] = piece(0).astype(mq_ref.dtype)
    mk_ref[...] = (piece(1) * (M_DK ** -0.5)).astype(mk_ref.dtype)
    mv_ref[...] = piece(2).astype(mv_ref.dtype)
    mo_ref[...] = piece(3)
    mz_ref[...] = piece(4)
    aq = rope512(piece(5), ch, sh, ROT_DIM // 2, LANES)
    aq_ref[...] = (aq * (A_DH ** -0.5)).astype(BF16)
    ak = rope512(piece(6), ch, sh, ROT_DIM // 2, LANES)
    ak_ref[...] = ak
    akb_ref[...] = ak.astype(BF16)
    av = piece(7)
    av_ref[...] = av
    avb_ref[...] = av.astype(BF16)
    az_ref[...] = piece(8)
    iq_ref[...] = rope512(piece(9), ci, si, IDX_ROT // 2, IDX_DIM).astype(BF16)
    psm = jnp.dot(x, ws_ref[...], preferred_element_type=F32) + bs_ref[...]
    lane = lax.broadcasted_iota(I32, psm.shape, 1)
    sm = psm
    for base, src_ref in ((_SM_QN, aq_ref), (_SM_KN, akb_ref)):
        for h in range(A_HEADS):
            v = src_ref[:, h * A_DH:(h + 1) * A_DH].astype(F32)
            sm = jnp.where(lane == base + h, jnp.sum(v * v, axis=-1, keepdims=True), sm)
    sm_ref[...] = sm
    ik = _rope128(psm, ci, si, IDX_ROT // 2, IDX_DIM)[:, :IDX_DIM]
    ik_ref[...] = ik
    ikb_ref[...] = ik.astype(BF16)
    gt = lax.dot_general(wg_ref[...], x, (((1,), (1,)), ((), ())), preferred_element_type=F32)
    gt_ref[...] = gt + bg_ref[...]


def _const_spec(shape):
    nd = len(shape)
    return pl.BlockSpec(shape, lambda *_: (0,) * nd, pipeline_mode=pl.Buffered(1))


def _project(x2d, pos, weights, n_batch, tm, qkv_dtype):
    wb, ws, wg, bb, bs, bg = weights
    n_rows = x2d.shape[0]
    n_pos = pos.shape[0]
    npb = n_pos // tm
    ch, sh = _rope_tables(pos, ROT_DIM, LANES)
    ci, si = _rope_tables(pos, IDX_ROT, IDX_DIM)

    row = lambda p, b: (b * npb + p, 0)
    tab = lambda p, b: (p, 0)
    wide = lambda dt: jax.ShapeDtypeStruct((n_rows, 512), dt)
    out_shape = (
        wide(qkv_dtype), wide(qkv_dtype), wide(qkv_dtype), wide(F32), wide(F32),
        wide(BF16), wide(F32), wide(F32), wide(F32),
        wide(BF16),
        jax.ShapeDtypeStruct((n_rows, IDX_DIM), F32),
        jax.ShapeDtypeStruct((n_rows, LANES), F32),
        jax.ShapeDtypeStruct((8, n_rows), F32),
        wide(BF16), wide(BF16),
        jax.ShapeDtypeStruct((n_rows, IDX_DIM), BF16),
    )
    out_specs = (
        [pl.BlockSpec((tm, 512), row)] * 10
        + [pl.BlockSpec((tm, IDX_DIM), row), pl.BlockSpec((tm, LANES), row),
           pl.BlockSpec((8, tm), lambda p, b: (0, b * npb + p)),
           pl.BlockSpec((tm, 512), row), pl.BlockSpec((tm, 512), row),
           pl.BlockSpec((tm, IDX_DIM), row)]
    )
    in_specs = [
        pl.BlockSpec((tm, D_MODEL), row),
        _const_spec(wb.shape), _const_spec(ws.shape), _const_spec(wg.shape),
        _const_spec(bb.shape), _const_spec(bs.shape), _const_spec(bg.shape),
        pl.BlockSpec((tm, LANES), tab), pl.BlockSpec((tm, LANES), tab),
        pl.BlockSpec((tm, LANES), tab), pl.BlockSpec((tm, LANES), tab),
    ]
    return pl.pallas_call(
        _proj_kernel,
        grid=(npb, n_batch),
        in_specs=in_specs,
        out_specs=out_specs,
        out_shape=out_shape,
        compiler_params=pltpu.CompilerParams(
            dimension_semantics=("arbitrary", "arbitrary"), vmem_limit_bytes=VMEM_LIMIT),
        name="proj_rope",
    )(x2d, wb, ws, wg, bb, bs, bg, ch, sh, ci, si)


def _prep_weights(w_in, b_in):
    offs = _offsets()
    col = lambda nm: w_in[:, offs[nm][0]:offs[nm][1]]
    bcol = lambda nm: b_in[offs[nm][0]:offs[nm][1]]
    wb = jnp.concatenate([col(nm) for nm in _BIG], axis=1).astype(BF16)
    bb = jnp.concatenate([bcol(nm) for nm in _BIG])[None, :]
    pad = LANES - (IDX_DIM + 2 * M_HEADS + IDX_HEADS)
    ws = jnp.concatenate([col("i_k"), col("m_i"), col("m_f"), col("i_w"),
                          jnp.zeros((D_MODEL, pad), w_in.dtype)], axis=1).astype(BF16)
    bs = jnp.concatenate([bcol("i_k"), bcol("m_i"), bcol("m_f"), bcol("i_w"),
                          jnp.zeros((pad,), b_in.dtype)])[None, :]
    wg = jnp.concatenate([col("m_i"), col("m_f")], axis=1).T.astype(BF16)
    bg = jnp.concatenate([bcol("m_i"), bcol("m_f")])[:, None]
    return wb, ws, wg, bb, bs, bg


def _cumsum_rows(x):
    n = x.shape[0]
    row = lax.broadcasted_iota(I32, x.shape, 0)
    s = 1
    while s < n:
        x = x + jnp.where(row >= s, pltpu.roll(x, s, axis=0), 0.0)
        s *= 2
    return x


def _cumsum_lanes(x):
    n = x.shape[1]
    lane = lax.broadcasted_iota(I32, x.shape, 1)
    s = 1
    while s < n:
        x = x + jnp.where(lane >= s, pltpu.roll(x, s, axis=1), 0.0)
        s *= 2
    return x


def _mlstm_kernel(mq_ref, mk_ref, mv_ref, sm_ref, gt_ref, h_ref, c_ref, n_ref, m_ref):
    c_idx = pl.program_id(1)
    L = mq_ref.shape[0]

    @pl.when(c_idx == 0)
    def _():
        c_ref[...] = jnp.zeros_like(c_ref)
        n_ref[...] = jnp.zeros_like(n_ref)
        m_ref[...] = jnp.zeros_like(m_ref)

    sm = sm_ref[...]
    gt = gt_ref[...]
    bcol = _cumsum_rows(_log_sigmoid(sm))
    brow = _cumsum_lanes(_log_sigmoid(gt))
    tpos = lax.broadcasted_iota(I32, (L, L), 0)
    spos = lax.broadcasted_iota(I32, (L, L), 1)
    causal = tpos >= spos

    for h in range(M_HEADS):
        hs = slice(h * M_DK, (h + 1) * M_DK)
        q = mq_ref[:, hs]
        k = mk_ref[:, hs]
        v = mv_ref[:, hs]
        li_c = sm[:, _SM_I + h:_SM_I + h + 1]
        b_c = bcol[:, _SM_F + h:_SM_F + h + 1]
        li_r = gt[h:h + 1, :]
        b_r = brow[M_HEADS + h:M_HEADS + h + 1, :]
        m_prev = m_ref[0, h][:, :1]
        c_prev = c_ref[0, h]
        n_prev = n_ref[0, h]

        dmat = jnp.where(causal, b_c + (li_r - b_r), -jnp.inf)
        inter = b_c + m_prev
        m_row = jnp.maximum(inter, jnp.max(dmat, axis=-1, keepdims=True))
        w_intra = jnp.exp(dmat - m_row)
        w_inter = jnp.exp(inter - m_row)
        s = lax.dot_general(q, k, (((1,), (1,)), ((), ())), preferred_element_type=F32)
        qk = s * w_intra
        num = (w_inter * jnp.dot(q, c_prev.astype(BF16), preferred_element_type=F32)
               + jnp.dot(qk.astype(BF16), v, preferred_element_type=F32))
        qf = q.astype(F32)
        den = (w_inter * jnp.sum(qf * n_prev, axis=-1, keepdims=True)
               + jnp.sum(qk, axis=-1, keepdims=True))
        h_ref[:, hs] = num / jnp.maximum(jnp.abs(den), jnp.exp(-m_row))

        b_last = b_c[L - 1:L, :]
        g_r = b_last - b_r + li_r
        m_new = jnp.maximum(b_last + m_prev, jnp.max(g_r, axis=-1, keepdims=True))
        decay = jnp.exp(b_last + m_prev - m_new)
        wk_c = jnp.exp(b_last - b_c + li_c - m_new)
        kf = k.astype(F32)
        wv = (wk_c * v.astype(F32)).astype(BF16)
        c_ref[0, h] = decay * c_prev + lax.dot_general(
            k, wv, (((0,), (0,)), ((), ())), preferred_element_type=F32)
        n_ref[0, h] = decay * n_prev + jnp.sum(wk_c * kf, axis=0, keepdims=True)
        m_ref[0, h] = jnp.broadcast_to(m_new, (1, LANES))


def _mlstm_prompt(mq, mk, mv, sm, gt, n_batch, seq, chunk):
    nc = seq // chunk
    row = lambda b, c: (b * nc + c, 0)
    st = lambda b, c: (b, 0, 0, 0)
    return pl.pallas_call(
        _mlstm_kernel,
        grid=(n_batch, nc),
        in_specs=[pl.BlockSpec((chunk, 512), row)] * 3
        + [pl.BlockSpec((chunk, LANES), row), pl.BlockSpec((8, chunk), lambda b, c: (0, b * nc + c))],
        out_specs=[pl.BlockSpec((chunk, 512), row),
                   pl.BlockSpec((1, M_HEADS, M_DK, M_DV), st),
                   pl.BlockSpec((1, M_HEADS, 1, M_DK), st),
                   pl.BlockSpec((1, M_HEADS, 1, LANES), st)],
        out_shape=(jax.ShapeDtypeStruct((n_batch * seq, 512), F32),
                   jax.ShapeDtypeStruct((n_batch, M_HEADS, M_DK, M_DV), F32),
                   jax.ShapeDtypeStruct((n_batch, M_HEADS, 1, M_DK), F32),
                   jax.ShapeDtypeStruct((n_batch, M_HEADS, 1, LANES), F32)),
        compiler_params=pltpu.CompilerParams(
            dimension_semantics=("arbitrary", "arbitrary"), vmem_limit_bytes=VMEM_LIMIT),
        name="mlstm_chunkwise",
    )(mq, mk, mv, sm, gt)


def _key_to_float(key):
    return pltpu.bitcast(key ^ ((key >> 31) & 0x7FFFFFFF), F32)


def _kth_largest(count_ge, shape, topk):
    def bit_step(i, key):
        cand = key + lax.shift_left(jnp.int32(1), 31 - i)
        return jnp.where(count_ge(_key_to_float(cand)) >= topk, cand, key)

    thr = _key_to_float(lax.fori_loop(0, 32, bit_step, jnp.full(shape, INT_MIN, I32)))
    return jnp.where(thr > -jnp.inf, thr, -jnp.inf)


def _dsa_prompt_kernel(iq_ref, sm_ref, aq_ref, kmax_ref, kidx_ref, k_ref, v_ref, out_ref,
                       sc_ref, wslab_ref, fslab_ref, islab_ref, m_ref, l_ref, acc_ref,
                       *, tq, tk, topk, idx_bits):
    qb = pl.program_id(1)
    ntile = ((qb + 1) * tq + tk - 1) // tk
    qpos = qb * tq + lax.broadcasted_iota(I32, (tq, 1), 0)
    lane_k = lax.broadcasted_iota(I32, (tq, tk), 1)
    wsc = sm_ref[:, _SM_W:_SM_W + IDX_HEADS] * (IDX_HEADS ** -0.5) * (IDX_DIM ** -0.5)
    q_idx = [iq_ref[:, h * IDX_DIM:(h + 1) * IDX_DIM] for h in range(IDX_HEADS)]

    ncol = tk // LANES
    cols = [slice(j * LANES, (j + 1) * LANES) for j in range(ncol)]
    for h in range(IDX_HEADS):
        wslab_ref[h] = jnp.broadcast_to(wsc[:, h:h + 1], (tq, LANES))

    def score_tile(t, carry):
        kt = kidx_ref[0, pl.ds(pl.multiple_of(t * tk, tk), tk), :]
        accs = [jnp.zeros((tq, LANES), F32) for _ in cols]
        for h in range(IDX_HEADS):
            d = lax.dot_general(q_idx[h], kt, (((1,), (1,)), ((), ())), preferred_element_type=F32)
            wh = wslab_ref[h]
            accs = [acc + wh * jnp.maximum(d[:, c], 0.0) for acc, c in zip(accs, cols)]
        kpos = t * tk + lane_k
        sc_ref[t] = jnp.where(kpos <= qpos, jnp.concatenate(accs, axis=1), -jnp.inf)
        return carry

    lax.fori_loop(0, ntile, score_tile, 0)

    @pl.when(ntile % 2 == 1)
    def _():
        sc_ref[ntile] = jnp.full((tq, tk), -jnp.inf, F32)

    rb = min(COUNT_ROWS, tq)
    lane128 = lax.broadcasted_iota(I32, (rb, LANES), 1)

    def set_f(x):
        fslab_ref[...] = jnp.broadcast_to(x, (tq, LANES))

    def set_i(x):
        islab_ref[...] = jnp.broadcast_to(x, (tq, LANES))

    def count(preds, use_pos=False):
        outs = [[] for _ in preds]
        for r0 in range(0, tq, rb):
            rows = slice(r0, r0 + rb)

            def body(p, accs, rows=rows):
                accs = list(accs)
                fv = fslab_ref[rows, :]
                iv = islab_ref[rows, :] if use_pos else None
                for t in (2 * p, 2 * p + 1):
                    for j, cs in enumerate(cols):
                        sc = sc_ref[t, rows, cs]
                        kpos = (t * tk + j * LANES) + lane128 if use_pos else None
                        for i, pred in enumerate(preds):
                            accs[i] = accs[i] + jnp.where(pred(sc, kpos, fv, iv), 1.0, 0.0)
                return tuple(accs)

            accs = lax.fori_loop(0, (ntile + 1) // 2, body,
                                 tuple(jnp.zeros((rb, LANES), F32) for _ in preds))
            for out, acc in zip(outs, accs):
                out.append(acc)
        ones = jnp.ones((LANES, LANES), BF16)
        return [jnp.dot(jnp.concatenate(out, axis=0).astype(BF16), ones, preferred_element_type=F32)
                for out in outs]

    def count_ge(c):
        set_f(c)
        return count([lambda sc, kp, fv, iv: sc >= fv])[0]

    thr = _kth_largest(count_ge, (tq, LANES), topk)
    set_f(thr)
    n_gt, n_ge = count([lambda sc, kp, fv, iv: sc > fv, lambda sc, kp, fv, iv: sc >= fv])
    need = topk - n_gt
    cut_ties = jnp.max(jnp.where(thr > -jnp.inf, n_ge - topk, 1.0)) > 0.0

    lane_q = lax.broadcasted_iota(I32, (tq, LANES), 1)

    def bias_simple():
        def body(t, carry):
            fv = fslab_ref[...]
            for cs in cols:
                sc_ref[t, :, cs] = jnp.where(sc_ref[t, :, cs] >= fv, 0.0, NEG_BIG)
            return carry
        lax.fori_loop(0, ntile, body, 0)

    def bias_cut_ties():
        def idx_step(i, jcut):
            cand = jcut + lax.shift_left(jnp.int32(1), idx_bits - 1 - i)
            set_i(cand)
            cnt, = count([lambda sc, kp, fv, iv: (sc == fv) & (kp < iv)], use_pos=True)
            return jnp.where(cnt < need, cand, jcut)

        jcut = lax.fori_loop(0, idx_bits, idx_step, jnp.zeros((tq, LANES), I32))
        set_i(jnp.where(thr > -jnp.inf, jcut, -1))

        def body(t, carry):
            fv = fslab_ref[...]
            iv = islab_ref[...]
            for j, cs in enumerate(cols):
                sc = sc_ref[t, :, cs]
                sel = (sc > fv) | ((sc == fv) & ((t * tk + j * LANES) + lane_q <= iv))
                sc_ref[t, :, cs] = jnp.where(sel, 0.0, NEG_BIG)
            return carry
        lax.fori_loop(0, ntile, body, 0)

    lax.cond(cut_ties, bias_cut_ties, bias_simple)

    def scores(t, h):
        hs = slice(h * A_DH, (h + 1) * A_DH)
        kh = k_ref[0, pl.ds(pl.multiple_of(t * tk, tk), tk), hs]
        s = lax.dot_general(aq_ref[:, hs], kh, (((1,), (1,)), ((), ())), preferred_element_type=F32)
        return s + sc_ref[t]

    l_ref[...] = jnp.zeros_like(l_ref)
    acc_ref[...] = jnp.zeros_like(acc_ref)

    def row_max():
        m_ref[...] = jnp.full_like(m_ref, NEG_BIG)

        def max_sweep(t, carry):
            for h in range(A_HEADS):
                s = scores(t, h)
                m = m_ref[h]
                for c in cols:
                    m = jnp.maximum(m, s[:, c])
                m_ref[h] = m
            return carry

        lax.fori_loop(0, ntile, max_sweep, 0)
        for h in range(A_HEADS):
            m_ref[h] = jnp.broadcast_to(jnp.max(m_ref[h], axis=-1, keepdims=True), (tq, LANES))

    def no_max():
        m_ref[...] = jnp.zeros_like(m_ref)

    bound_sq = sm_ref[:, _SM_QN:_SM_QN + A_HEADS] * kmax_ref[0]
    lax.cond(jnp.max(bound_sq) <= SCORE_BOUND_SQ, no_max, row_max)

    def sum_sweep(t, carry):
        start = pl.multiple_of(t * tk, tk)
        for h in range(A_HEADS):
            s = scores(t, h)
            mb = m_ref[h]
            ps = [jnp.exp(s[:, c] - mb) for c in cols]
            lsum = ps[0]
            for pj in ps[1:]:
                lsum = lsum + pj
            l_ref[h] = l_ref[h] + lsum
            vh = v_ref[0, pl.ds(start, tk), h * A_DH:(h + 1) * A_DH]
            p = jnp.concatenate(ps, axis=1).astype(BF16)
            acc_ref[h] = acc_ref[h] + jnp.dot(p, vh, preferred_element_type=F32)
        return carry

    lax.fori_loop(0, ntile, sum_sweep, 0)
    for h in range(A_HEADS):
        out_ref[:, h * A_DH:(h + 1) * A_DH] = acc_ref[h] / jnp.sum(l_ref[h], axis=-1, keepdims=True)


def _dsa_prompt(iq, sm, aq, ikb, akb, avb, n_batch, seq, tq, tk):
    nq = seq // tq
    topk = min(TOPK_MAX, seq // 4)
    row = lambda b, q: (b * nq + q, 0)
    full = lambda b, q: (b, 0, 0)
    kern = functools.partial(_dsa_prompt_kernel, tq=tq, tk=tk, topk=topk,
                             idx_bits=(seq - 1).bit_length())
    kmax = jnp.max(sm[:, _SM_KN:_SM_KN + A_HEADS].reshape(n_batch, seq, A_HEADS), axis=1, keepdims=True)
    return pl.pallas_call(
        kern,
        grid=(n_batch, nq),
        in_specs=[pl.BlockSpec((tq, 512), row), pl.BlockSpec((tq, LANES), row),
                  pl.BlockSpec((tq, 512), row), pl.BlockSpec((1, 1, A_HEADS), full),
                  pl.BlockSpec((1, seq, IDX_DIM), full, pipeline_mode=pl.Buffered(1)),
                  pl.BlockSpec((1, seq, 512), full, pipeline_mode=pl.Buffered(1)),
                  pl.BlockSpec((1, seq, 512), full, pipeline_mode=pl.Buffered(1))],
        out_specs=pl.BlockSpec((tq, 512), row),
        out_shape=jax.ShapeDtypeStruct((n_batch * seq, 512), F32),
        scratch_shapes=[pltpu.VMEM((seq // tk + 1, tq, tk), F32),
                        pltpu.VMEM((IDX_HEADS, tq, LANES), F32),
                        pltpu.VMEM((tq, LANES), F32), pltpu.VMEM((tq, LANES), I32),
                        pltpu.VMEM((A_HEADS, tq, LANES), F32), pltpu.VMEM((A_HEADS, tq, LANES), F32),
                        pltpu.VMEM((A_HEADS, tq, A_DH), F32)],
        compiler_params=pltpu.CompilerParams(
            dimension_semantics=("arbitrary", "arbitrary"), vmem_limit_bytes=VMEM_LIMIT),
        name="dsa_prompt",
    )(iq, sm, aq, kmax, ikb.reshape(n_batch, seq, IDX_DIM), akb.reshape(n_batch, seq, 512),
      avb.reshape(n_batch, seq, 512))


def _out_kernel(x_ref, hm_ref, mo_ref, mz_ref, ha_ref, az_ref, gain_ref, wo_ref, g_ref, b_ref,
                y_ref, *, alpha):
    hm = hm_ref[...] * _sigmoid(mo_ref[...])
    parts = []
    for h in range(M_HEADS):
        hh = hm[:, h * M_DV:(h + 1) * M_DV]
        mu = jnp.mean(hh, axis=-1, keepdims=True)
        var = jnp.mean(jnp.square(hh - mu), axis=-1, keepdims=True)
        parts.append((hh - mu) * lax.rsqrt(var + LN_EPS))
    mz = mz_ref[...]
    az = az_ref[...]
    g_m = jnp.concatenate(parts, axis=-1) * gain_ref[...] * (mz * _sigmoid(mz))
    g_a = ha_ref[...] * (az * _sigmoid(az))
    mixed = jnp.concatenate([g_m, g_a], axis=-1).astype(BF16)
    sub = jnp.dot(mixed, wo_ref[...], preferred_element_type=F32)
    r = alpha * x_ref[...] + sub
    mu = jnp.mean(r, axis=-1, keepdims=True)
    var = jnp.mean(jnp.square(r - mu), axis=-1, keepdims=True)
    y_ref[...] = (r - mu) * lax.rsqrt(var + LN_EPS) * g_ref[...] + b_ref[...]


def _out_mix(x2d, hm, mo, mz, ha, az, gain, wo, ln_g, ln_b, alpha, tm):
    n_rows = x2d.shape[0]
    row = lambda i: (i, 0)
    return pl.pallas_call(
        functools.partial(_out_kernel, alpha=alpha),
        grid=(n_rows // tm,),
        in_specs=[pl.BlockSpec((tm, D_MODEL), row)] + [pl.BlockSpec((tm, 512), row)] * 5
        + [_const_spec((1, M_WIDTH)), _const_spec((MIX_WIDTH, D_MODEL)),
           _const_spec((1, D_MODEL)), _const_spec((1, D_MODEL))],
        out_specs=pl.BlockSpec((tm, D_MODEL), row),
        out_shape=jax.ShapeDtypeStruct((n_rows, D_MODEL), F32),
        compiler_params=pltpu.CompilerParams(
            dimension_semantics=("arbitrary",), vmem_limit_bytes=VMEM_LIMIT),
        name="out_mix",
    )(x2d, hm, mo, mz, ha, az, gain[None, :], wo.astype(BF16), ln_g[None, :], ln_b[None, :])


def _mlstm_step_kernel(qc_ref, kc_ref, qr_ref, kr_ref, vr_ref, i_ref, f_ref, c0_ref, n0_ref, m0_ref,
                       h_ref, c_ref, n_ref, m_ref):
    lf = _log_sigmoid(f_ref[...])
    m0 = m0_ref[...]
    it = i_ref[...]
    m_new = jnp.maximum(lf + m0, it)
    fg = jnp.exp(lf + m0 - m_new)
    ig = jnp.exp(it - m_new)
    c_new = fg * c0_ref[...] + (ig * kc_ref[...]) * vr_ref[...]
    n_new = fg * n0_ref[...] + ig * kr_ref[...]
    num = jnp.sum(qc_ref[...] * c_new, axis=2, keepdims=True)
    den = jnp.sum(qr_ref[...] * n_new, axis=3, keepdims=True)
    h_ref[...] = num / jnp.maximum(jnp.abs(den), jnp.exp(-m_new))
    c_ref[...] = c_new
    n_ref[...] = n_new
    m_ref[...] = m_new


def _mlstm_step(mq, mk, mv, sm, c0, n0, m0, tb):
    db = mq.shape[0]
    col = lambda t: t.reshape(db, M_HEADS, M_DK, 1)
    rowv = lambda t: t.reshape(db, M_HEADS, 1, M_DK)
    sc = lambda t: t.reshape(db, M_HEADS, 1, 1)
    idx = lambda i: (i, 0, 0, 0)
    s_col = pl.BlockSpec((tb, M_HEADS, M_DK, 1), idx)
    s_row = pl.BlockSpec((tb, M_HEADS, 1, M_DK), idx)
    s_sc = pl.BlockSpec((tb, M_HEADS, 1, 1), idx)
    s_mat = pl.BlockSpec((tb, M_HEADS, M_DK, M_DV), idx)
    return pl.pallas_call(
        _mlstm_step_kernel,
        grid=(db // tb,),
        in_specs=[s_col, s_col, s_row, s_row, s_row, s_sc, s_sc, s_mat, s_row, s_sc],
        out_specs=[s_row, s_mat, s_row, s_sc],
        out_shape=(jax.ShapeDtypeStruct((db, M_HEADS, 1, M_DV), F32),
                   jax.ShapeDtypeStruct((db, M_HEADS, M_DK, M_DV), F32),
                   jax.ShapeDtypeStruct((db, M_HEADS, 1, M_DK), F32),
                   jax.ShapeDtypeStruct((db, M_HEADS, 1, 1), F32)),
        compiler_params=pltpu.CompilerParams(
            dimension_semantics=("arbitrary",), vmem_limit_bytes=VMEM_LIMIT),
        name="mlstm_step",
    )(col(mq), col(mk), rowv(mq), rowv(mk), rowv(mv),
      sc(sm[:, _SM_I:_SM_I + M_HEADS]), sc(sm[:, _SM_F:_SM_F + M_HEADS]),
      c0, rowv(n0), sc(m0))


SCORE_ROWS_PAD = 8


def _samp_index_kernel(pt_ref, q_ref, w_ref, knew_ref, kidx_hbm, out_ref, buf, sem, *, n_pages):
    b = pl.program_id(0)
    nb = pl.num_programs(0)
    slot = b % 2

    def copies(seq, sl):
        return [pltpu.make_async_copy(kidx_hbm.at[pt_ref[seq, p]], buf.at[sl, p], sem.at[sl])
                for p in range(n_pages)]

    @pl.when(b == 0)
    def _():
        for cp in copies(0, 0):
            cp.start()

    @pl.when(b + 1 < nb)
    def _():
        for cp in copies(b + 1, 1 - slot):
            cp.start()

    for cp in copies(b, slot):
        cp.wait()

    q = q_ref[0]
    wcol = w_ref[0] * (IDX_HEADS ** -0.5) * (IDX_DIM ** -0.5)

    def group(g, carry):
        rows = []
        for j in range(8):
            kt = buf[slot, g * 8 + j].astype(BF16)
            d = jnp.dot(q, kt, preferred_element_type=F32)
            rows.append(jnp.sum(jnp.maximum(d, 0.0) * wcol, axis=0, keepdims=True))
        out_ref[0, pl.ds(pl.multiple_of(g * 8, 8), 8), :] = jnp.concatenate(rows, axis=0)
        return carry

    lax.fori_loop(0, n_pages // 8, group, 0)
    kn = knew_ref[0].astype(BF16).astype(F32)
    dn = jnp.sum(q.astype(F32) * kn, axis=-1, keepdims=True)
    scn = jnp.sum(jnp.maximum(dn, 0.0) * wcol, axis=0, keepdims=True)
    lane = lax.broadcasted_iota(I32, (SCORE_ROWS_PAD, LANES), 1)
    row = lax.broadcasted_iota(I32, (SCORE_ROWS_PAD, LANES), 0)
    out_ref[0, n_pages:n_pages + SCORE_ROWS_PAD, :] = jnp.where((lane == 0) & (row == 0), scn, -jnp.inf)


def _samp_index(page_table, q, wcol, knew, kidx_t):
    db, n_pages = page_table.shape
    assert n_pages % 8 == 0
    rows = n_pages + SCORE_ROWS_PAD
    per_seq = lambda shape: pl.BlockSpec((1,) + shape, lambda b, pt: (b, 0, 0))
    grid_spec = pltpu.PrefetchScalarGridSpec(
        num_scalar_prefetch=1,
        grid=(db,),
        in_specs=[per_seq((IDX_HEADS, IDX_DIM)), per_seq((IDX_HEADS, 1)), per_seq((1, IDX_DIM)),
                  pl.BlockSpec(memory_space=pl.ANY)],
        out_specs=per_seq((rows, LANES)),
        scratch_shapes=[pltpu.VMEM((2, n_pages, IDX_DIM, PAGE_SIZE), F32),
                        pltpu.SemaphoreType.DMA((2,))],
    )
    return pl.pallas_call(
        functools.partial(_samp_index_kernel, n_pages=n_pages),
        grid_spec=grid_spec,
        out_shape=jax.ShapeDtypeStruct((db, rows, LANES), F32),
        compiler_params=pltpu.CompilerParams(
            dimension_semantics=("arbitrary",), vmem_limit_bytes=VMEM_LIMIT),
        name="samp_index",
    )(page_table, q, wcol, knew, kidx_t)


def _samp_topk_kernel(sc_ref, pt_ref, rid_ref, meta_ref, *, topk, idx_bits, n_pages):
    sc = sc_ref[...]
    tb, rows, _ = sc.shape
    kpos = (lax.broadcasted_iota(I32, sc.shape, 1) * PAGE_SIZE
            + lax.broadcasted_iota(I32, sc.shape, 2))

    def count(m):
        c = jnp.sum(jnp.where(m, 1.0, 0.0), axis=1, keepdims=True)
        return jnp.sum(c, axis=2, keepdims=True)

    thr = _kth_largest(lambda c: count(sc >= c), (tb, 1, 1), topk)
    need = topk - count(sc > thr)

    def idx_step(i, jcut):
        cand = jcut + lax.shift_left(jnp.int32(1), idx_bits - 1 - i)
        return jnp.where(count((sc == thr) & (kpos < cand)) < need, cand, jcut)

    jcut = lax.fori_loop(0, idx_bits, idx_step, jnp.zeros((tb, 1, 1), I32))
    jcut = jnp.where(thr > -jnp.inf, jcut, -1)
    sel = jnp.where((sc > thr) | ((sc == thr) & (kpos <= jcut)), 1.0, 0.0)

    slot = lax.broadcasted_iota(I32, (topk, LANES), 0).astype(F32)
    lane = lax.broadcasted_iota(I32, (topk, LANES), 1)
    lane1 = lax.broadcasted_iota(I32, (1, LANES), 1)
    ones8 = jnp.ones((8, LANES), BF16)
    zpad = jnp.zeros((LANES - rows, LANES), F32)
    for i in range(tb):
        sel_i = jnp.concatenate([sel[i], zpad], axis=0)
        incl = _cumsum_lanes(sel_i)
        tot = lax.dot_general(ones8, sel_i.astype(BF16), (((1,), (1,)), ((), ())),
                              preferred_element_type=F32)[0:1, :]
        end = _cumsum_lanes(tot)
        start = end - tot
        rj = jnp.sum(jnp.where(end <= slot, 1.0, 0.0), axis=-1, keepdims=True)
        onehot = jnp.where(lane.astype(F32) == rj, 1.0, 0.0)
        local = slot[:, :1] - jnp.sum(onehot * start, axis=-1, keepdims=True)
        incl_j = jnp.dot(onehot.astype(BF16), incl.astype(BF16), preferred_element_type=F32)
        off = jnp.sum(jnp.where(incl_j <= local, 1.0, 0.0), axis=-1, keepdims=True)
        phys = jnp.sum(onehot * pt_ref[i], axis=-1, keepdims=True)
        rid_ref[i] = (phys * PAGE_SIZE + off).astype(I32)
        n_past = end[:, n_pages - 1:n_pages]
        new_sel = tot[:, n_pages:n_pages + 1]
        meta_ref[i] = jnp.where(lane1 == 0, n_past, jnp.where(lane1 == 1, new_sel, 0.0)).astype(I32)


def _samp_topk(sc, ptf, n_keys, n_pages, tb):
    db, rows, _ = sc.shape
    topk = min(TOPK_MAX, n_keys // 4)
    blk = lambda shape: pl.BlockSpec((tb,) + shape, lambda i: (i, 0, 0))
    return pl.pallas_call(
        functools.partial(_samp_topk_kernel, topk=topk, idx_bits=(n_keys - 1).bit_length(),
                          n_pages=n_pages),
        grid=(db // tb,),
        in_specs=[blk((rows, LANES)), blk((1, LANES))],
        out_specs=[blk((topk, 1)), blk((1, LANES))],
        out_shape=(jax.ShapeDtypeStruct((db, topk, 1), I32), jax.ShapeDtypeStruct((db, 1, LANES), I32)),
        compiler_params=pltpu.CompilerParams(
            dimension_semantics=("arbitrary",), vmem_limit_bytes=VMEM_LIMIT),
        name="samp_topk",
    )(sc, ptf)


PAGE_SHIFT = PAGE_SIZE.bit_length() - 1


def _samp_attn_kernel(rid_ref, meta_ref, q_ref, knew_ref, vnew_ref, k_hbm, v_hbm, out_ref,
                      kbuf, vbuf, ksem, vsem, *, layer, nsel):
    b = pl.program_id(0)
    nb = pl.num_programs(0)
    slot = b % 2

    def row_copies(page, off, sl, j):
        return (pltpu.make_async_copy(k_hbm.at[layer, page, off], kbuf.at[sl, j], ksem.at[sl]),
                pltpu.make_async_copy(v_hbm.at[layer, page, off], vbuf.at[sl, j], vsem.at[sl]))

    def issue(seq, sl):
        def body(j, carry):
            rid = rid_ref[seq, j]
            page = lax.shift_right_logical(rid, jnp.int32(PAGE_SHIFT))
            for cp in row_copies(page, rid & (PAGE_SIZE - 1), sl, j):
                cp.start()
            return carry
        lax.fori_loop(0, nsel, body, 0, unroll=8)

    @pl.when(b == 0)
    def _():
        issue(0, 0)

    @pl.when(b + 1 < nb)
    def _():
        issue(b + 1, 1 - slot)

    def wait_row(j, carry):
        for cp in row_copies(0, 0, slot, j):
            cp.wait()
        return carry
    lax.fori_loop(0, nsel, wait_row, 0, unroll=8)

    n_past = meta_ref[b, 0]
    has_new = meta_ref[b, 1] > 0
    q = q_ref[...]
    s = jnp.sum(kbuf[slot] * q, axis=-1, keepdims=True)
    valid = lax.broadcasted_iota(I32, s.shape, 0) < n_past
    s = jnp.where(valid, s, NEG_BIG)
    s_new = jnp.sum(knew_ref[...] * q, axis=-1, keepdims=True)
    m = jnp.max(s, axis=0, keepdims=True)
    m = jnp.where(has_new, jnp.maximum(m, s_new), m)
    p = jnp.where(valid, jnp.exp(s - m), 0.0)
    p_new = jnp.where(has_new, jnp.exp(s_new - m), 0.0)
    l = jnp.sum(p, axis=0, keepdims=True) + p_new
    o = jnp.sum(p * vbuf[slot], axis=0, keepdims=True) + p_new * vnew_ref[...]
    out_ref[...] = o / l


def _samp_attn(rid, meta, q, knew, vnew, k_cache, v_cache, layer):
    db, nsel = rid.shape
    per_seq = pl.BlockSpec((1, A_HEADS, A_DH), lambda b, r, m: (b, 0, 0))
    grid_spec = pltpu.PrefetchScalarGridSpec(
        num_scalar_prefetch=2,
        grid=(db,),
        in_specs=[per_seq, per_seq, per_seq,
                  pl.BlockSpec(memory_space=pl.ANY), pl.BlockSpec(memory_space=pl.ANY)],
        out_specs=per_seq,
        scratch_shapes=[pltpu.VMEM((2, nsel, A_HEADS, A_DH), F32),
                        pltpu.VMEM((2, nsel, A_HEADS, A_DH), F32),
                        pltpu.SemaphoreType.DMA((2,)), pltpu.SemaphoreType.DMA((2,))],
    )
    return pl.pallas_call(
        functools.partial(_samp_attn_kernel, layer=layer, nsel=nsel),
        grid_spec=grid_spec,
        out_shape=jax.ShapeDtypeStruct((db, A_HEADS, A_DH), F32),
        compiler_params=pltpu.CompilerParams(
            dimension_semantics=("arbitrary",), vmem_limit_bytes=VMEM_LIMIT),
        name="samp_attn",
    )(rid, meta, q, knew, vnew, k_cache, v_cache)


def kernel(x_prompt, x_sample, state_C, state_n, state_m, cache_k, cache_v, cache_kidx, page_table,
           w_in, b_in, mh_gain, w_out, ln_g, ln_b):
    depth = w_in.shape[0]
    n_batch, seq, _ = x_prompt.shape
    db, dec_seq, _ = x_sample.shape
    assert dec_seq == 1, "one new token per sampled sequence"
    n_pages = page_table.shape[1]
    past = n_pages * PAGE_SIZE
    alpha = (2.0 * depth) ** 0.25
    pos_p = jnp.arange(seq)
    pos_s = past + jnp.arange(dec_seq)
    tm = min(512, seq)
    chunk = min(256, seq)
    tq, tk = min(512, seq), min(256, seq)

    hp = x_prompt.reshape(n_batch * seq, D_MODEL)
    hs = x_sample.reshape(db * dec_seq, D_MODEL)
    outs = {k: [] for k in ("kp", "vp", "ip", "Cp", "np", "mp", "ks", "vs", "is", "Cs", "ns", "ms")}
    for l in range(depth):
        weights = _prep_weights(w_in[l], b_in[l])
        (mq, mk, mv, mo, mz, aq, ak, av, az, iq, ik, sm, gt, akb, avb, ikb) = _project(
            hp, pos_p, weights, n_batch, tm, BF16)
        h_m, c_p, n_p, m_p = _mlstm_prompt(mq, mk, mv, sm, gt, n_batch, seq, chunk)
        h_a = _dsa_prompt(iq, sm, aq, ikb, akb, avb, n_batch, seq, tq, tk)
        hp_new = _out_mix(hp, h_m, mo, mz, h_a, az, mh_gain[l], w_out[l], ln_g[l], ln_b[l], alpha, tm)
        outs["kp"].append(ak.reshape(n_batch, seq, A_HEADS, A_DH))
        outs["vp"].append(av.reshape(n_batch, seq, A_HEADS, A_DH))
        outs["ip"].append(ik.reshape(n_batch, seq, IDX_DIM))
        outs["Cp"].append(c_p)
        outs["np"].append(n_p.reshape(n_batch, M_HEADS, M_DK))
        outs["mp"].append(m_p[:, :, 0, 0])
        hp = hp_new
        (smq, smk, smv, smo, smz, saq, sak, sav, saz, siq, sik, ssm, _, _, _, _) = _project(
            hs, jnp.tile(pos_s, db), weights, 1, min(128, db * dec_seq), F32)
        g_m, c_s, n_s, m_s = _mlstm_step(smq, smk, smv, ssm, state_C[l], state_n[l], state_m[l],
                                         tb=min(8, db))
        kidx_t = jnp.swapaxes(cache_kidx[l], 1, 2)
        sc = _samp_index(page_table, siq.reshape(db, IDX_HEADS, IDX_DIM),
                         ssm[:, _SM_W:_SM_W + IDX_HEADS, None], sik[:, None, :], kidx_t)
        ptf = jnp.pad(page_table.astype(F32), ((0, 0), (0, LANES - n_pages)))[:, None, :]
        rid, meta = _samp_topk(sc, ptf, past + dec_seq, n_pages, tb=min(8, db))
        heads = lambda t: t.astype(F32).reshape(db, A_HEADS, A_DH)
        g_a = _samp_attn(rid[:, :, 0], meta[:, 0, :2], heads(saq), heads(sak), heads(sav),
                         cache_k, cache_v, l)
        hs_new = _out_mix(hs, g_m.reshape(db, M_WIDTH), smo, smz, g_a.reshape(db, A_WIDTH), saz,
                          mh_gain[l], w_out[l], ln_g[l], ln_b[l], alpha, min(128, db))
        outs["ks"].append(sak.reshape(db, dec_seq, A_HEADS, A_DH))
        outs["vs"].append(sav.reshape(db, dec_seq, A_HEADS, A_DH))
        outs["is"].append(sik.reshape(db, dec_seq, IDX_DIM))
        outs["Cs"].append(c_s)
        outs["ns"].append(n_s.reshape(db, M_HEADS, M_DK))
        outs["ms"].append(m_s.reshape(db, M_HEADS))
        hs = hs_new
    st = lambda k: jnp.stack(outs[k])
    return (hp.reshape(n_batch, seq, D_MODEL), hs.reshape(db, dec_seq, D_MODEL),
            st("kp"), st("vp"), st("ip"), st("Cp"), st("np"), st("mp"),
            st("ks"), st("vs"), st("is"), st("Cs"), st("ns"), st("ms"))
```

```python
import functools

import jax
import jax.numpy as jnp
from jax import lax
from jax.experimental import pallas as pl
from jax.experimental.pallas import tpu as pltpu

F32 = jnp.float32
BF16 = jnp.bfloat16
I32 = jnp.int32

D_MODEL = 1024
PAGE_SIZE = 128
M_HEADS = 4
M_DK = 128
M_DV = 128
M_WIDTH = M_HEADS * M_DV
A_HEADS = 4
A_DH = 128
A_WIDTH = A_HEADS * A_DH
ROT_DIM = A_DH // 4
ROPE_THETA = 500000.0
IDX_HEADS = 8
IDX_DIM = 64
IDX_ROT = IDX_DIM // 4
TOPK_MAX = 256
MIX_WIDTH = M_WIDTH + A_WIDTH
LN_EPS = 1e-5

LANES = 128
NEG_BIG = -1e30
INT_MIN = -(2 ** 31)

_SPLITS = (
    ("m_q", M_WIDTH), ("m_k", M_WIDTH), ("m_v", M_WIDTH), ("m_o", M_WIDTH), ("m_z", M_WIDTH),
    ("m_i", M_HEADS), ("m_f", M_HEADS),
    ("a_q", A_WIDTH), ("a_k", A_WIDTH), ("a_v", A_WIDTH), ("a_z", A_WIDTH),
    ("i_q", IDX_HEADS * IDX_DIM), ("i_k", IDX_DIM), ("i_w", IDX_HEADS),
)
_BIG = ("m_q", "m_k", "m_v", "m_o", "m_z", "a_q", "a_k", "a_v", "a_z", "i_q")
_SM_I = IDX_DIM
_SM_F = IDX_DIM + M_HEADS
_SM_W = IDX_DIM + 2 * M_HEADS
_SM_QN = _SM_W + IDX_HEADS
_SM_KN = _SM_QN + A_HEADS
SCORE_BOUND_SQ = 1400.0
COUNT_ROWS = 128
COUNT_KEYS = 512
VMEM_LIMIT = 56 * 1024 * 1024


def _offsets():
    offs, start = {}, 0
    for name, n in _SPLITS:
        offs[name] = (start, start + n)
        start += n
    return offs


def _log_sigmoid(x):
    return jnp.minimum(x, 0.0) - jnp.log1p(jnp.exp(-jnp.abs(x)))


def _sigmoid(x):
    return 1.0 / (1.0 + jnp.exp(-x))


def _rope_tables(pos, rot_dim, period):
    half = rot_dim // 2
    inv = ROPE_THETA ** (-jnp.arange(half, dtype=F32) / half)
    ang = pos.astype(F32)[:, None] * inv
    cos, sin = jnp.cos(ang), jnp.sin(ang)
    n = pos.shape[0]
    c = jnp.concatenate([cos, cos, jnp.ones((n, period - rot_dim), F32)], axis=-1)
    s = jnp.concatenate([sin, sin, jnp.zeros((n, period - rot_dim), F32)], axis=-1)
    reps = LANES // period
    return jnp.tile(c, (1, reps)), jnp.tile(s, (1, reps))


def _rope128(x, c, s, half, period):
    lane = lax.broadcasted_iota(I32, x.shape, 1) % period
    up = pltpu.roll(x, LANES - half, axis=1)
    dn = pltpu.roll(x, half, axis=1)
    t = jnp.where(lane < half, -up, dn)
    return x * c + t * s


def _proj_kernel(x_ref, wb_ref, ws_ref, wg_ref, bb_ref, bs_ref, bg_ref,
                 ch_ref, sh_ref, ci_ref, si_ref,
                 mq_ref, mk_ref, mv_ref, mo_ref, mz_ref, aq_ref, ak_ref, av_ref, az_ref,
                 iq_ref, ik_ref, sm_ref, gt_ref, akb_ref, avb_ref, ikb_ref):
    x = x_ref[...].astype(BF16)

    def piece(j):
        w = wb_ref[:, j * 512:(j + 1) * 512]
        return jnp.dot(x, w, preferred_element_type=F32) + bb_ref[:, j * 512:(j + 1) * 512]

    def rope512(p, c, s, half, period):
        return jnp.concatenate(
            [_rope128(p[:, g * LANES:(g + 1) * LANES], c, s, half, period) for g in range(4)], axis=-1)

    ch, sh = ch_ref[...], sh_ref[...]
    ci, si = ci_ref[...], si_ref[...]
    mq_ref[...] = piece(0).astype(mq_ref.dtype)
    mk_ref[...] = (piece(1) * (M_DK ** -0.5)).astype(mk_ref.dtype)
    mv_ref[...] = piece(2).astype(mv_ref.dtype)
    mo_ref[...] = piece(3)
    mz_ref[...] = piece(4)
    aq = rope512(piece(5), ch, sh, ROT_DIM // 2, LANES)
    aq_ref[...] = (aq * (A_DH ** -0.5)).astype(BF16)
    ak = rope512(piece(6), ch, sh, ROT_DIM // 2, LANES)
    ak_ref[...] = ak
    akb_ref[...] = ak.astype(BF16)
    av = piece(7)
    av_ref[...] = av
    avb_ref[...] = av.astype(BF16)
    az_ref[...] = piece(8)
    iq_ref[...] = rope512(piece(9), ci, si, IDX_ROT // 2, IDX_DIM).astype(BF16)
    psm = jnp.dot(x, ws_ref[...], preferred_element_type=F32) + bs_ref[...]
    lane = lax.broadcasted_iota(I32, psm.shape, 1)
    sm = psm
    for base, src_ref in ((_SM_QN, aq_ref), (_SM_KN, akb_ref)):
        for h in range(A_HEADS):
            v = src_ref[:, h * A_DH:(h + 1) * A_DH].astype(F32)
            sm = jnp.where(lane == base + h, jnp.sum(v * v, axis=-1, keepdims=True), sm)
    sm_ref[...] = sm
    ik = _rope128(psm, ci, si, IDX_ROT // 2, IDX_DIM)[:, :IDX_DIM]
    ik_ref[...] = ik
    ikb_ref[...] = ik.astype(BF16)
    gt = lax.dot_general(wg_ref[...], x, (((1,), (1,)), ((), ())), preferred_element_type=F32)
    gt_ref[...] = gt + bg_ref[...]


def _const_spec(shape):
    nd = len(shape)
    return pl.BlockSpec(shape, lambda *_: (0,) * nd, pipeline_mode=pl.Buffered(1))


def _project(x2d, pos, weights, n_batch, tm, qkv_dtype):
    wb, ws, wg, bb, bs, bg = weights
    n_rows = x2d.shape[0]
    n_pos = pos.shape[0]
    npb = n_pos // tm
    ch, sh = _rope_tables(pos, ROT_DIM, LANES)
    ci, si = _rope_tables(pos, IDX_ROT, IDX_DIM)

    row = lambda p, b: (b * npb + p, 0)
    tab = lambda p, b: (p, 0)
    wide = lambda dt: jax.ShapeDtypeStruct((n_rows, 512), dt)
    out_shape = (
        wide(qkv_dtype), wide(qkv_dtype), wide(qkv_dtype), wide(F32), wide(F32),
        wide(BF16), wide(F32), wide(F32), wide(F32),
        wide(BF16),
        jax.ShapeDtypeStruct((n_rows, IDX_DIM), F32),
        jax.ShapeDtypeStruct((n_rows, LANES), F32),
        jax.ShapeDtypeStruct((8, n_rows), F32),
        wide(BF16), wide(BF16),
        jax.ShapeDtypeStruct((n_rows, IDX_DIM), BF16),
    )
    out_specs = (
        [pl.BlockSpec((tm, 512), row)] * 10
        + [pl.BlockSpec((tm, IDX_DIM), row), pl.BlockSpec((tm, LANES), row),
           pl.BlockSpec((8, tm), lambda p, b: (0, b * npb + p)),
           pl.BlockSpec((tm, 512), row), pl.BlockSpec((tm, 512), row),
           pl.BlockSpec((tm, IDX_DIM), row)]
    )
    in_specs = [
        pl.BlockSpec((tm, D_MODEL), row),
        _const_spec(wb.shape), _const_spec(ws.shape), _const_spec(wg.shape),
        _const_spec(bb.shape), _const_spec(bs.shape), _const_spec(bg.shape),
        pl.BlockSpec((tm, LANES), tab), pl.BlockSpec((tm, LANES), tab),
        pl.BlockSpec((tm, LANES), tab), pl.BlockSpec((tm, LANES), tab),
    ]
    return pl.pallas_call(
        _proj_kernel,
        grid=(npb, n_batch),
        in_specs=in_specs,
        out_specs=out_specs,
        out_shape=out_shape,
        compiler_params=pltpu.CompilerParams(
            dimension_semantics=("arbitrary", "arbitrary"), vmem_limit_bytes=VMEM_LIMIT),
        name="proj_rope",
    )(x2d, wb, ws, wg, bb, bs, bg, ch, sh, ci, si)


def _prep_weights(w_in, b_in):
    offs = _offsets()
    col = lambda nm: w_in[:, offs[nm][0]:offs[nm][1]]
    bcol = lambda nm: b_in[offs[nm][0]:offs[nm][1]]
    wb = jnp.concatenate([col(nm) for nm in _BIG], axis=1).astype(BF16)
    bb = jnp.concatenate([bcol(nm) for nm in _BIG])[None, :]
    pad = LANES - (IDX_DIM + 2 * M_HEADS + IDX_HEADS)
    ws = jnp.concatenate([col("i_k"), col("m_i"), col("m_f"), col("i_w"),
                          jnp.zeros((D_MODEL, pad), w_in.dtype)], axis=1).astype(BF16)
    bs = jnp.concatenate([bcol("i_k"), bcol("m_i"), bcol("m_f"), bcol("i_w"),
                          jnp.zeros((pad,), b_in.dtype)])[None, :]
    wg = jnp.concatenate([col("m_i"), col("m_f")], axis=1).T.astype(BF16)
    bg = jnp.concatenate([bcol("m_i"), bcol("m_f")])[:, None]
    return wb, ws, wg, bb, bs, bg


def _cumsum_rows(x):
    n = x.shape[0]
    row = lax.broadcasted_iota(I32, x.shape, 0)
    s = 1
    while s < n:
        x = x + jnp.where(row >= s, pltpu.roll(x, s, axis=0), 0.0)
        s *= 2
    return x


def _cumsum_lanes(x):
    n = x.shape[1]
    lane = lax.broadcasted_iota(I32, x.shape, 1)
    s = 1
    while s < n:
        x = x + jnp.where(lane >= s, pltpu.roll(x, s, axis=1), 0.0)
        s *= 2
    return x


def _mlstm_kernel(mq_ref, mk_ref, mv_ref, sm_ref, gt_ref, h_ref, c_ref, n_ref, m_ref):
    c_idx = pl.program_id(1)
    L = mq_ref.shape[0]

    @pl.when(c_idx == 0)
    def _():
        c_ref[...] = jnp.zeros_like(c_ref)
        n_ref[...] = jnp.zeros_like(n_ref)
        m_ref[...] = jnp.zeros_like(m_ref)

    sm = sm_ref[...]
    gt = gt_ref[...]
    bcol = _cumsum_rows(_log_sigmoid(sm))
    brow = _cumsum_lanes(_log_sigmoid(gt))
    tpos = lax.broadcasted_iota(I32, (L, L), 0)
    spos = lax.broadcasted_iota(I32, (L, L), 1)
    causal = tpos >= spos

    for h in range(M_HEADS):
        hs = slice(h * M_DK, (h + 1) * M_DK)
        q = mq_ref[:, hs]
        k = mk_ref[:, hs]
        v = mv_ref[:, hs]
        li_c = sm[:, _SM_I + h:_SM_I + h + 1]
        b_c = bcol[:, _SM_F + h:_SM_F + h + 1]
        li_r = gt[h:h + 1, :]
        b_r = brow[M_HEADS + h:M_HEADS + h + 1, :]
        m_prev = m_ref[0, h][:, :1]
        c_prev = c_ref[0, h]
        n_prev = n_ref[0, h]

        dmat = jnp.where(causal, b_c + (li_r - b_r), -jnp.inf)
        inter = b_c + m_prev
        m_row = jnp.maximum(inter, jnp.max(dmat, axis=-1, keepdims=True))
        w_intra = jnp.exp(dmat - m_row)
        w_inter = jnp.exp(inter - m_row)
        s = lax.dot_general(q, k, (((1,), (1,)), ((), ())), preferred_element_type=F32)
        qk = s * w_intra
        num = (w_inter * jnp.dot(q, c_prev.astype(BF16), preferred_element_type=F32)
               + jnp.dot(qk.astype(BF16), v, preferred_element_type=F32))
        qf = q.astype(F32)
        den = (w_inter * jnp.sum(qf * n_prev, axis=-1, keepdims=True)
               + jnp.sum(qk, axis=-1, keepdims=True))
        h_ref[:, hs] = num / jnp.maximum(jnp.abs(den), jnp.exp(-m_row))

        b_last = b_c[L - 1:L, :]
        g_r = b_last - b_r + li_r
        m_new = jnp.maximum(b_last + m_prev, jnp.max(g_r, axis=-1, keepdims=True))
        decay = jnp.exp(b_last + m_prev - m_new)
        wk_c = jnp.exp(b_last - b_c + li_c - m_new)
        kf = k.astype(F32)
        wv = (wk_c * v.astype(F32)).astype(BF16)
        c_ref[0, h] = decay * c_prev + lax.dot_general(
            k, wv, (((0,), (0,)), ((), ())), preferred_element_type=F32)
        n_ref[0, h] = decay * n_prev + jnp.sum(wk_c * kf, axis=0, keepdims=True)
        m_ref[0, h] = jnp.broadcast_to(m_new, (1, LANES))


def _mlstm_prompt(mq, mk, mv, sm, gt, n_batch, seq, chunk):
    nc = seq // chunk
    row = lambda b, c: (b * nc + c, 0)
    st = lambda b, c: (b, 0, 0, 0)
    return pl.pallas_call(
        _mlstm_kernel,
        grid=(n_batch, nc),
        in_specs=[pl.BlockSpec((chunk, 512), row)] * 3
        + [pl.BlockSpec((chunk, LANES), row), pl.BlockSpec((8, chunk), lambda b, c: (0, b * nc + c))],
        out_specs=[pl.BlockSpec((chunk, 512), row),
                   pl.BlockSpec((1, M_HEADS, M_DK, M_DV), st),
                   pl.BlockSpec((1, M_HEADS, 1, M_DK), st),
                   pl.BlockSpec((1, M_HEADS, 1, LANES), st)],
        out_shape=(jax.ShapeDtypeStruct((n_batch * seq, 512), F32),
                   jax.ShapeDtypeStruct((n_batch, M_HEADS, M_DK, M_DV), F32),
                   jax.ShapeDtypeStruct((n_batch, M_HEADS, 1, M_DK), F32),
                   jax.ShapeDtypeStruct((n_batch, M_HEADS, 1, LANES), F32)),
        compiler_params=pltpu.CompilerParams(
            dimension_semantics=("arbitrary", "arbitrary"), vmem_limit_bytes=VMEM_LIMIT),
        name="mlstm_chunkwise",
    )(mq, mk, mv, sm, gt)


def _key_to_float(key):
    return pltpu.bitcast(key ^ ((key >> 31) & 0x7FFFFFFF), F32)


def _kth_largest(count_ge, shape, topk):
    def bit_step(i, carry):
        key, n_ge = carry
        cand = key + lax.shift_left(jnp.int32(1), 31 - i)
        cnt = count_ge(_key_to_float(cand))
        ok = cnt >= topk
        return jnp.where(ok, cand, key), jnp.where(ok, cnt, n_ge)

    key, n_ge = lax.fori_loop(0, 32, bit_step,
                              (jnp.full(shape, INT_MIN, I32), jnp.full(shape, 2.0 ** 30, F32)))
    thr = _key_to_float(key)
    return jnp.where(thr > -jnp.inf, thr, -jnp.inf), n_ge


def _dsa_prompt_kernel(iq_ref, sm_ref, aq_ref, kmax_ref, kidx_ref, k_ref, v_ref, out_ref,
                       sc_ref, wslab_ref, fslab_ref, islab_ref, m_ref, l_ref, acc_ref,
                       *, tq, tk, topk, idx_bits):
    qb = pl.program_id(1)
    ntile = ((qb + 1) * tq + tk - 1) // tk
    qpos = qb * tq + lax.broadcasted_iota(I32, (tq, 1), 0)
    lane_k = lax.broadcasted_iota(I32, (tq, tk), 1)
    wsc = sm_ref[:, _SM_W:_SM_W + IDX_HEADS] * (IDX_HEADS ** -0.5) * (IDX_DIM ** -0.5)
    q_idx = [iq_ref[:, h * IDX_DIM:(h + 1) * IDX_DIM] for h in range(IDX_HEADS)]

    ncol = tk // LANES
    cols = [slice(j * LANES, (j + 1) * LANES) for j in range(ncol)]
    for h in range(IDX_HEADS):
        wslab_ref[h] = jnp.broadcast_to(wsc[:, h:h + 1], (tq, LANES))

    def score_tile(t, carry):
        kt = kidx_ref[0, pl.ds(pl.multiple_of(t * tk, tk), tk), :]
        accs = [jnp.zeros((tq, LANES), F32) for _ in cols]
        for h in range(IDX_HEADS):
            d = lax.dot_general(q_idx[h], kt, (((1,), (1,)), ((), ())), preferred_element_type=F32)
            wh = wslab_ref[h]
            accs = [acc + wh * jnp.maximum(d[:, c], 0.0) for acc, c in zip(accs, cols)]
        kpos = t * tk + lane_k
        sc_ref[t] = jnp.where(kpos <= qpos, jnp.concatenate(accs, axis=1), -jnp.inf)
        return carry

    lax.fori_loop(0, ntile, score_tile, 0)

    tpi = max(1, COUNT_KEYS // tk)
    niter = (ntile + tpi - 1) // tpi
    for extra in range(tpi - 1):
        @pl.when(ntile + extra < niter * tpi)
        def _():
            sc_ref[ntile + extra] = jnp.full((tq, tk), -jnp.inf, F32)

    rb = min(COUNT_ROWS, tq)
    lane128 = lax.broadcasted_iota(I32, (rb, LANES), 1)

    def set_f(x):
        fslab_ref[...] = jnp.broadcast_to(x, (tq, LANES))

    def set_i(x):
        islab_ref[...] = jnp.broadcast_to(x, (tq, LANES))

    def count(preds, use_pos=False):
        outs = [[] for _ in preds]
        for r0 in range(0, tq, rb):
            rows = slice(r0, r0 + rb)

            def body(p, accs, rows=rows):
                accs = list(accs)
                fv = fslab_ref[rows, :]
                iv = islab_ref[rows, :] if use_pos else None
                for t in [tpi * p + i for i in range(tpi)]:
                    for j, cs in enumerate(cols):
                        sc = sc_ref[t, rows, cs]
                        kpos = (t * tk + j * LANES) + lane128 if use_pos else None
                        for i, pred in enumerate(preds):
                            accs[i] = accs[i] + jnp.where(pred(sc, kpos, fv, iv), 1.0, 0.0)
                return tuple(accs)

            accs = lax.fori_loop(0, niter, body,
                                 tuple(jnp.zeros((rb, LANES), F32) for _ in preds))
            for out, acc in zip(outs, accs):
                out.append(acc)
        ones = jnp.ones((LANES, LANES), BF16)
        return [jnp.dot(jnp.concatenate(out, axis=0).astype(BF16), ones, preferred_element_type=F32)
                for out in outs]

    def count_ge(c):
        set_f(c)
        return count([lambda sc, kp, fv, iv: sc >= fv])[0]

    thr, n_ge = _kth_largest(count_ge, (tq, LANES), topk)
    set_f(thr)
    cut_ties = jnp.max(jnp.where(thr > -jnp.inf, n_ge - topk, 1.0)) > 0.0

    lane_q = lax.broadcasted_iota(I32, (tq, LANES), 1)

    def bias_simple():
        def body(t, carry):
            fv = fslab_ref[...]
            for cs in cols:
                sc_ref[t, :, cs] = jnp.where(sc_ref[t, :, cs] >= fv, 0.0, NEG_BIG)
            return carry
        lax.fori_loop(0, ntile, body, 0)

    def bias_cut_ties():
        n_gt, = count([lambda sc, kp, fv, iv: sc > fv])
        need = topk - n_gt

        def idx_step(i, jcut):
            cand = jcut + lax.shift_left(jnp.int32(1), idx_bits - 1 - i)
            set_i(cand)
            cnt, = count([lambda sc, kp, fv, iv: (sc == fv) & (kp < iv)], use_pos=True)
            return jnp.where(cnt < need, cand, jcut)

        jcut = lax.fori_loop(0, idx_bits, idx_step, jnp.zeros((tq, LANES), I32))
        set_i(jnp.where(thr > -jnp.inf, jcut, -1))

        def body(t, carry):
            fv = fslab_ref[...]
            iv = islab_ref[...]
            for j, cs in enumerate(cols):
                sc = sc_ref[t, :, cs]
                sel = (sc > fv) | ((sc == fv) & ((t * tk + j * LANES) + lane_q <= iv))
                sc_ref[t, :, cs] = jnp.where(sel, 0.0, NEG_BIG)
            return carry
        lax.fori_loop(0, ntile, body, 0)

    lax.cond(cut_ties, bias_cut_ties, bias_simple)

    def scores(t, h):
        hs = slice(h * A_DH, (h + 1) * A_DH)
        kh = k_ref[0, pl.ds(pl.multiple_of(t * tk, tk), tk), hs]
        s = lax.dot_general(aq_ref[:, hs], kh, (((1,), (1,)), ((), ())), preferred_element_type=F32)
        return s + sc_ref[t]

    l_ref[...] = jnp.zeros_like(l_ref)
    acc_ref[...] = jnp.zeros_like(acc_ref)

    def row_max():
        m_ref[...] = jnp.full_like(m_ref, NEG_BIG)

        def max_sweep(t, carry):
            for h in range(A_HEADS):
                s = scores(t, h)
                m = m_ref[h]
                for c in cols:
                    m = jnp.maximum(m, s[:, c])
                m_ref[h] = m
            return carry

        lax.fori_loop(0, ntile, max_sweep, 0)
        for h in range(A_HEADS):
            m_ref[h] = jnp.broadcast_to(jnp.max(m_ref[h], axis=-1, keepdims=True), (tq, LANES))

    def no_max():
        m_ref[...] = jnp.zeros_like(m_ref)

    bound_sq = sm_ref[:, _SM_QN:_SM_QN + A_HEADS] * kmax_ref[0]
    lax.cond(jnp.max(bound_sq) <= SCORE_BOUND_SQ, no_max, row_max)

    def sum_sweep(t, carry):
        start = pl.multiple_of(t * tk, tk)
        for h in range(A_HEADS):
            s = scores(t, h)
            mb = m_ref[h]
            ps = [jnp.exp(s[:, c] - mb) for c in cols]
            lsum = ps[0]
            for pj in ps[1:]:
                lsum = lsum + pj
            l_ref[h] = l_ref[h] + lsum
            vh = v_ref[0, pl.ds(start, tk), h * A_DH:(h + 1) * A_DH]
            p = jnp.concatenate(ps, axis=1).astype(BF16)
            acc_ref[h] = acc_ref[h] + jnp.dot(p, vh, preferred_element_type=F32)
        return carry

    lax.fori_loop(0, ntile, sum_sweep, 0)
    for h in range(A_HEADS):
        out_ref[:, h * A_DH:(h + 1) * A_DH] = acc_ref[h] / jnp.sum(l_ref[h], axis=-1, keepdims=True)


def _dsa_prompt(iq, sm, aq, ikb, akb, avb, n_batch, seq, tq, tk):
    nq = seq // tq
    topk = min(TOPK_MAX, seq // 4)
    row = lambda b, q: (b * nq + q, 0)
    full = lambda b, q: (b, 0, 0)
    kern = functools.partial(_dsa_prompt_kernel, tq=tq, tk=tk, topk=topk,
                             idx_bits=(seq - 1).bit_length())
    kmax = jnp.max(sm[:, _SM_KN:_SM_KN + A_HEADS].reshape(n_batch, seq, A_HEADS), axis=1, keepdims=True)
    return pl.pallas_call(
        kern,
        grid=(n_batch, nq),
        in_specs=[pl.BlockSpec((tq, 512), row), pl.BlockSpec((tq, LANES), row),
                  pl.BlockSpec((tq, 512), row), pl.BlockSpec((1, 1, A_HEADS), full),
                  pl.BlockSpec((1, seq, IDX_DIM), full, pipeline_mode=pl.Buffered(1)),
                  pl.BlockSpec((1, seq, 512), full, pipeline_mode=pl.Buffered(1)),
                  pl.BlockSpec((1, seq, 512), full, pipeline_mode=pl.Buffered(1))],
        out_specs=pl.BlockSpec((tq, 512), row),
        out_shape=jax.ShapeDtypeStruct((n_batch * seq, 512), F32),
        scratch_shapes=[pltpu.VMEM((seq // tk + max(1, COUNT_KEYS // tk) - 1, tq, tk), F32),
                        pltpu.VMEM((IDX_HEADS, tq, LANES), F32),
                        pltpu.VMEM((tq, LANES), F32), pltpu.VMEM((tq, LANES), I32),
                        pltpu.VMEM((A_HEADS, tq, LANES), F32), pltpu.VMEM((A_HEADS, tq, LANES), F32),
                        pltpu.VMEM((A_HEADS, tq, A_DH), F32)],
        compiler_params=pltpu.CompilerParams(
            dimension_semantics=("arbitrary", "arbitrary"), vmem_limit_bytes=VMEM_LIMIT),
        name="dsa_prompt",
    )(iq, sm, aq, kmax, ikb.reshape(n_batch, seq, IDX_DIM), akb.reshape(n_batch, seq, 512),
      avb.reshape(n_batch, seq, 512))


def _out_kernel(x_ref, hm_ref, mo_ref, mz_ref, ha_ref, az_ref, gain_ref, wo_ref, g_ref, b_ref,
                y_ref, *, alpha):
    hm = hm_ref[...] * _sigmoid(mo_ref[...])
    parts = []
    for h in range(M_HEADS):
        hh = hm[:, h * M_DV:(h + 1) * M_DV]
        mu = jnp.mean(hh, axis=-1, keepdims=True)
        var = jnp.mean(jnp.square(hh - mu), axis=-1, keepdims=True)
        parts.append((hh - mu) * lax.rsqrt(var + LN_EPS))
    mz = mz_ref[...]
    az = az_ref[...]
    g_m = jnp.concatenate(parts, axis=-1) * gain_ref[...] * (mz * _sigmoid(mz))
    g_a = ha_ref[...] * (az * _sigmoid(az))
    mixed = jnp.concatenate([g_m, g_a], axis=-1).astype(BF16)
    sub = jnp.dot(mixed, wo_ref[...], preferred_element_type=F32)
    r = alpha * x_ref[...] + sub
    mu = jnp.mean(r, axis=-1, keepdims=True)
    var = jnp.mean(jnp.square(r - mu), axis=-1, keepdims=True)
    y_ref[...] = (r - mu) * lax.rsqrt(var + LN_EPS) * g_ref[...] + b_ref[...]


def _out_mix(x2d, hm, mo, mz, ha, az, gain, wo, ln_g, ln_b, alpha, tm):
    n_rows = x2d.shape[0]
    row = lambda i: (i, 0)
    return pl.pallas_call(
        functools.partial(_out_kernel, alpha=alpha),
        grid=(n_rows // tm,),
        in_specs=[pl.BlockSpec((tm, D_MODEL), row)] + [pl.BlockSpec((tm, 512), row)] * 5
        + [_const_spec((1, M_WIDTH)), _const_spec((MIX_WIDTH, D_MODEL)),
           _const_spec((1, D_MODEL)), _const_spec((1, D_MODEL))],
        out_specs=pl.BlockSpec((tm, D_MODEL), row),
        out_shape=jax.ShapeDtypeStruct((n_rows, D_MODEL), F32),
        compiler_params=pltpu.CompilerParams(
            dimension_semantics=("arbitrary",), vmem_limit_bytes=VMEM_LIMIT),
        name="out_mix",
    )(x2d, hm, mo, mz, ha, az, gain[None, :], wo.astype(BF16), ln_g[None, :], ln_b[None, :])


def _mlstm_step_kernel(qc_ref, kc_ref, qr_ref, kr_ref, vr_ref, i_ref, f_ref, c0_ref, n0_ref, m0_ref,
                       h_ref, c_ref, n_ref, m_ref):
    lf = _log_sigmoid(f_ref[...])
    m0 = m0_ref[...]
    it = i_ref[...]
    m_new = jnp.maximum(lf + m0, it)
    fg = jnp.exp(lf + m0 - m_new)
    ig = jnp.exp(it - m_new)
    c_new = fg * c0_ref[...] + (ig * kc_ref[...]) * vr_ref[...]
    n_new = fg * n0_ref[...] + ig * kr_ref[...]
    num = jnp.sum(qc_ref[...] * c_new, axis=2, keepdims=True)
    den = jnp.sum(qr_ref[...] * n_new, axis=3, keepdims=True)
    h_ref[...] = num / jnp.maximum(jnp.abs(den), jnp.exp(-m_new))
    c_ref[...] = c_new
    n_ref[...] = n_new
    m_ref[...] = m_new


def _mlstm_step(mq, mk, mv, sm, c0, n0, m0, tb):
    db = mq.shape[0]
    col = lambda t: t.reshape(db, M_HEADS, M_DK, 1)
    rowv = lambda t: t.reshape(db, M_HEADS, 1, M_DK)
    sc = lambda t: t.reshape(db, M_HEADS, 1, 1)
    idx = lambda i: (i, 0, 0, 0)
    s_col = pl.BlockSpec((tb, M_HEADS, M_DK, 1), idx)
    s_row = pl.BlockSpec((tb, M_HEADS, 1, M_DK), idx)
    s_sc = pl.BlockSpec((tb, M_HEADS, 1, 1), idx)
    s_mat = pl.BlockSpec((tb, M_HEADS, M_DK, M_DV), idx)
    return pl.pallas_call(
        _mlstm_step_kernel,
        grid=(db // tb,),
        in_specs=[s_col, s_col, s_row, s_row, s_row, s_sc, s_sc, s_mat, s_row, s_sc],
        out_specs=[s_row, s_mat, s_row, s_sc],
        out_shape=(jax.ShapeDtypeStruct((db, M_HEADS, 1, M_DV), F32),
                   jax.ShapeDtypeStruct((db, M_HEADS, M_DK, M_DV), F32),
                   jax.ShapeDtypeStruct((db, M_HEADS, 1, M_DK), F32),
                   jax.ShapeDtypeStruct((db, M_HEADS, 1, 1), F32)),
        compiler_params=pltpu.CompilerParams(
            dimension_semantics=("arbitrary",), vmem_limit_bytes=VMEM_LIMIT),
        name="mlstm_step",
    )(col(mq), col(mk), rowv(mq), rowv(mk), rowv(mv),
      sc(sm[:, _SM_I:_SM_I + M_HEADS]), sc(sm[:, _SM_F:_SM_F + M_HEADS]),
      c0, rowv(n0), sc(m0))


SCORE_ROWS_PAD = 8


def _samp_index_kernel(pt_ref, q_ref, w_ref, knew_ref, kidx_hbm, out_ref, buf, sem, *, n_pages):
    b = pl.program_id(0)
    nb = pl.num_programs(0)
    slot = b % 2

    def copies(seq, sl):
        return [pltpu.make_async_copy(kidx_hbm.at[pt_ref[seq, p]], buf.at[sl, p], sem.at[sl])
                for p in range(n_pages)]

    @pl.when(b == 0)
    def _():
        for cp in copies(0, 0):
            cp.start()

    @pl.when(b + 1 < nb)
    def _():
        for cp in copies(b + 1, 1 - slot):
            cp.start()

    for cp in copies(b, slot):
        cp.wait()

    q = q_ref[0]
    wcol = w_ref[0] * (IDX_HEADS ** -0.5) * (IDX_DIM ** -0.5)

    gp = 16 if n_pages % 16 == 0 else 8

    def group(g, carry):
        rows = []
        for j in range(gp):
            kt = buf[slot, g * gp + j].astype(BF16)
            d = jnp.dot(q, kt, preferred_element_type=F32)
            rows.append(jnp.sum(jnp.maximum(d, 0.0) * wcol, axis=0, keepdims=True))
        out_ref[0, pl.ds(pl.multiple_of(g * gp, gp), gp), :] = jnp.concatenate(rows, axis=0)
        return carry

    lax.fori_loop(0, n_pages // gp, group, 0)
    kn = knew_ref[0].astype(BF16).astype(F32)
    dn = jnp.sum(q.astype(F32) * kn, axis=-1, keepdims=True)
    scn = jnp.sum(jnp.maximum(dn, 0.0) * wcol, axis=0, keepdims=True)
    lane = lax.broadcasted_iota(I32, (SCORE_ROWS_PAD, LANES), 1)
    row = lax.broadcasted_iota(I32, (SCORE_ROWS_PAD, LANES), 0)
    out_ref[0, n_pages:n_pages + SCORE_ROWS_PAD, :] = jnp.where((lane == 0) & (row == 0), scn, -jnp.inf)


def _samp_index(page_table, q, wcol, knew, kidx_t):
    db, n_pages = page_table.shape
    assert n_pages % 8 == 0
    rows = n_pages + SCORE_ROWS_PAD
    per_seq = lambda shape: pl.BlockSpec((1,) + shape, lambda b, pt: (b, 0, 0))
    grid_spec = pltpu.PrefetchScalarGridSpec(
        num_scalar_prefetch=1,
        grid=(db,),
        in_specs=[per_seq((IDX_HEADS, IDX_DIM)), per_seq((IDX_HEADS, 1)), per_seq((1, IDX_DIM)),
                  pl.BlockSpec(memory_space=pl.ANY)],
        out_specs=per_seq((rows, LANES)),
        scratch_shapes=[pltpu.VMEM((2, n_pages, IDX_DIM, PAGE_SIZE), F32),
                        pltpu.SemaphoreType.DMA((2,))],
    )
    return pl.pallas_call(
        functools.partial(_samp_index_kernel, n_pages=n_pages),
        grid_spec=grid_spec,
        out_shape=jax.ShapeDtypeStruct((db, rows, LANES), F32),
        compiler_params=pltpu.CompilerParams(
            dimension_semantics=("arbitrary",), vmem_limit_bytes=VMEM_LIMIT),
        name="samp_index",
    )(page_table, q, wcol, knew, kidx_t)


def _samp_topk_kernel(sc_ref, pt_ref, rid_ref, meta_ref, *, topk, idx_bits, n_pages):
    sc = sc_ref[...]
    tb, rows, _ = sc.shape
    kpos = (lax.broadcasted_iota(I32, sc.shape, 1) * PAGE_SIZE
            + lax.broadcasted_iota(I32, sc.shape, 2))

    def count(m):
        c = jnp.sum(jnp.where(m, 1.0, 0.0), axis=1, keepdims=True)
        return jnp.sum(c, axis=2, keepdims=True)

    thr, _ = _kth_largest(lambda c: count(sc >= c), (tb, 1, 1), topk)
    need = topk - count(sc > thr)

    def idx_step(i, jcut):
        cand = jcut + lax.shift_left(jnp.int32(1), idx_bits - 1 - i)
        return jnp.where(count((sc == thr) & (kpos < cand)) < need, cand, jcut)

    jcut = lax.fori_loop(0, idx_bits, idx_step, jnp.zeros((tb, 1, 1), I32))
    jcut = jnp.where(thr > -jnp.inf, jcut, -1)
    sel = jnp.where((sc > thr) | ((sc == thr) & (kpos <= jcut)), 1.0, 0.0)

    slot = lax.broadcasted_iota(I32, (topk, LANES), 0).astype(F32)
    lane = lax.broadcasted_iota(I32, (topk, LANES), 1)
    lane1 = lax.broadcasted_iota(I32, (1, LANES), 1)
    ones8 = jnp.ones((8, LANES), BF16)
    zpad = jnp.zeros((LANES - rows, LANES), F32)
    for i in range(tb):
        sel_i = jnp.concatenate([sel[i], zpad], axis=0)
        incl = _cumsum_lanes(sel_i)
        tot = lax.dot_general(ones8, sel_i.astype(BF16), (((1,), (1,)), ((), ())),
                              preferred_element_type=F32)[0:1, :]
        end = _cumsum_lanes(tot)
        start = end - tot
        rj = jnp.sum(jnp.where(end <= slot, 1.0, 0.0), axis=-1, keepdims=True)
        onehot = jnp.where(lane.astype(F32) == rj, 1.0, 0.0)
        local = slot[:, :1] - jnp.sum(onehot * start, axis=-1, keepdims=True)
        incl_j = jnp.dot(onehot.astype(BF16), incl.astype(BF16), preferred_element_type=F32)
        off = jnp.sum(jnp.where(incl_j <= local, 1.0, 0.0), axis=-1, keepdims=True)
        phys = jnp.sum(onehot * pt_ref[i], axis=-1, keepdims=True)
        rid_ref[i] = (phys * PAGE_SIZE + off).astype(I32)
        n_past = end[:, n_pages - 1:n_pages]
        new_sel = tot[:, n_pages:n_pages + 1]
        meta_ref[i] = jnp.where(lane1 == 0, n_past, jnp.where(lane1 == 1, new_sel, 0.0)).astype(I32)


def _samp_topk(sc, ptf, n_keys, n_pages, tb):
    db, rows, _ = sc.shape
    topk = min(TOPK_MAX, n_keys // 4)
    blk = lambda shape: pl.BlockSpec((tb,) + shape, lambda i: (i, 0, 0))
    return pl.pallas_call(
        functools.partial(_samp_topk_kernel, topk=topk, idx_bits=(n_keys - 1).bit_length(),
                          n_pages=n_pages),
        grid=(db // tb,),
        in_specs=[blk((rows, LANES)), blk((1, LANES))],
        out_specs=[blk((topk, 1)), blk((1, LANES))],
        out_shape=(jax.ShapeDtypeStruct((db, topk, 1), I32), jax.ShapeDtypeStruct((db, 1, LANES), I32)),
        compiler_params=pltpu.CompilerParams(
            dimension_semantics=("arbitrary",), vmem_limit_bytes=VMEM_LIMIT),
        name="samp_topk",
    )(sc, ptf)


PAGE_SHIFT = PAGE_SIZE.bit_length() - 1


def _samp_attn_kernel(rid_ref, meta_ref, q_ref, knew_ref, vnew_ref, k_hbm, v_hbm, out_ref,
                      kbuf, vbuf, ksem, vsem, *, layer, nsel):
    b = pl.program_id(0)
    nb = pl.num_programs(0)
    slot = b % 2

    def row_copies(page, off, sl, j):
        return (pltpu.make_async_copy(k_hbm.at[layer, page, off], kbuf.at[sl, j], ksem.at[sl]),
                pltpu.make_async_copy(v_hbm.at[layer, page, off], vbuf.at[sl, j], vsem.at[sl]))

    def issue(seq, sl):
        def body(j, carry):
            rid = rid_ref[seq, j]
            page = lax.shift_right_logical(rid, jnp.int32(PAGE_SHIFT))
            for cp in row_copies(page, rid & (PAGE_SIZE - 1), sl, j):
                cp.start()
            return carry
        lax.fori_loop(0, nsel, body, 0, unroll=8)

    @pl.when(b == 0)
    def _():
        issue(0, 0)

    @pl.when(b + 1 < nb)
    def _():
        issue(b + 1, 1 - slot)

    pltpu.make_async_copy(kbuf.at[slot], kbuf.at[slot], ksem.at[slot]).wait()
    pltpu.make_async_copy(vbuf.at[slot], vbuf.at[slot], vsem.at[slot]).wait()

    n_past = meta_ref[b, 0]
    has_new = meta_ref[b, 1] > 0
    q = q_ref[...]
    s = jnp.sum(kbuf[slot] * q, axis=-1, keepdims=True)
    valid = lax.broadcasted_iota(I32, s.shape, 0) < n_past
    s = jnp.where(valid, s, NEG_BIG)
    s_new = jnp.sum(knew_ref[...] * q, axis=-1, keepdims=True)
    m = jnp.max(s, axis=0, keepdims=True)
    m = jnp.where(has_new, jnp.maximum(m, s_new), m)
    p = jnp.where(valid, jnp.exp(s - m), 0.0)
    p_new = jnp.where(has_new, jnp.exp(s_new - m), 0.0)
    l = jnp.sum(p, axis=0, keepdims=True) + p_new
    o = jnp.sum(p * vbuf[slot], axis=0, keepdims=True) + p_new * vnew_ref[...]
    out_ref[...] = o / l


def _samp_attn(rid, meta, q, knew, vnew, k_cache, v_cache, layer):
    db, nsel = rid.shape
    per_seq = pl.BlockSpec((1, A_HEADS, A_DH), lambda b, r, m: (b, 0, 0))
    grid_spec = pltpu.PrefetchScalarGridSpec(
        num_scalar_prefetch=2,
        grid=(db,),
        in_specs=[per_seq, per_seq, per_seq,
                  pl.BlockSpec(memory_space=pl.ANY), pl.BlockSpec(memory_space=pl.ANY)],
        out_specs=per_seq,
        scratch_shapes=[pltpu.VMEM((2, nsel, A_HEADS, A_DH), F32),
                        pltpu.VMEM((2, nsel, A_HEADS, A_DH), F32),
                        pltpu.SemaphoreType.DMA((2,)), pltpu.SemaphoreType.DMA((2,))],
    )
    return pl.pallas_call(
        functools.partial(_samp_attn_kernel, layer=layer, nsel=nsel),
        grid_spec=grid_spec,
        out_shape=jax.ShapeDtypeStruct((db, A_HEADS, A_DH), F32),
        compiler_params=pltpu.CompilerParams(
            dimension_semantics=("arbitrary",), vmem_limit_bytes=VMEM_LIMIT),
        name="samp_attn",
    )(rid, meta, q, knew, vnew, k_cache, v_cache)


def kernel(x_prompt, x_sample, state_C, state_n, state_m, cache_k, cache_v, cache_kidx, page_table,
           w_in, b_in, mh_gain, w_out, ln_g, ln_b):
    depth = w_in.shape[0]
    n_batch, seq, _ = x_prompt.shape
    db, dec_seq, _ = x_sample.shape
    assert dec_seq == 1, "one new token per sampled sequence"
    n_pages = page_table.shape[1]
    past = n_pages * PAGE_SIZE
    alpha = (2.0 * depth) ** 0.25
    pos_p = jnp.arange(seq)
    pos_s = past + jnp.arange(dec_seq)
    tm = min(512, seq)
    chunk = min(512, seq)
    tq, tk = min(512, seq), min(512, seq)

    hp = x_prompt.reshape(n_batch * seq, D_MODEL)
    hs = x_sample.reshape(db * dec_seq, D_MODEL)
    outs = {k: [] for k in ("kp", "vp", "ip", "Cp", "np", "mp", "ks", "vs", "is", "Cs", "ns", "ms")}
    for l in range(depth):
        weights = _prep_weights(w_in[l], b_in[l])
        (mq, mk, mv, mo, mz, aq, ak, av, az, iq, ik, sm, gt, akb, avb, ikb) = _project(
            hp, pos_p, weights, n_batch, tm, BF16)
        h_m, c_p, n_p, m_p = _mlstm_prompt(mq, mk, mv, sm, gt, n_batch, seq, chunk)
        h_a = _dsa_prompt(iq, sm, aq, ikb, akb, avb, n_batch, seq, tq, tk)
        hp_new = _out_mix(hp, h_m, mo, mz, h_a, az, mh_gain[l], w_out[l], ln_g[l], ln_b[l], alpha, tm)
        outs["kp"].append(ak.reshape(n_batch, seq, A_HEADS, A_DH))
        outs["vp"].append(av.reshape(n_batch, seq, A_HEADS, A_DH))
        outs["ip"].append(ik.reshape(n_batch, seq, IDX_DIM))
        outs["Cp"].append(c_p)
        outs["np"].append(n_p.reshape(n_batch, M_HEADS, M_DK))
        outs["mp"].append(m_p[:, :, 0, 0])
        hp = hp_new
        (smq, smk, smv, smo, smz, saq, sak, sav, saz, siq, sik, ssm, _, _, _, _) = _project(
            hs, jnp.tile(pos_s, db), weights, 1, min(128, db * dec_seq), F32)
        g_m, c_s, n_s, m_s = _mlstm_step(smq, smk, smv, ssm, state_C[l], state_n[l], state_m[l],
                                         tb=min(8, db))
        kidx_t = jnp.swapaxes(cache_kidx[l], 1, 2)
        sc = _samp_index(page_table, siq.reshape(db, IDX_HEADS, IDX_DIM),
                         ssm[:, _SM_W:_SM_W + IDX_HEADS, None], sik[:, None, :], kidx_t)
        ptf = jnp.pad(page_table.astype(F32), ((0, 0), (0, LANES - n_pages)))[:, None, :]
        rid, meta = _samp_topk(sc, ptf, past + dec_seq, n_pages, tb=min(8, db))
        heads = lambda t: t.astype(F32).reshape(db, A_HEADS, A_DH)
        g_a = _samp_attn(rid[:, :, 0], meta[:, 0, :2], heads(saq), heads(sak), heads(sav),
                         cache_k, cache_v, l)
        hs_new = _out_mix(hs, g_m.reshape(db, M_WIDTH), smo, smz, g_a.reshape(db, A_WIDTH), saz,
                          mh_gain[l], w_out[l], ln_g[l], ln_b[l], alpha, min(128, db))
        outs["ks"].append(sak.reshape(db, dec_seq, A_HEADS, A_DH))
        outs["vs"].append(sav.reshape(db, dec_seq, A_HEADS, A_DH))
        outs["is"].append(sik.reshape(db, dec_seq, IDX_DIM))
        outs["Cs"].append(c_s)
        outs["ns"].append(n_s.reshape(db, M_HEADS, M_DK))
        outs["ms"].append(m_s.reshape(db, M_HEADS))
        hs = hs_new
    st = lambda k: jnp.stack(outs[k])
    return (hp.reshape(n_batch, seq, D_MODEL), hs.reshape(db, dec_seq, D_MODEL),
            st("kp"), st("vp"), st("ip"), st("Cp"), st("np"), st("mp"),
            st("ks"), st("vs"), st("is"), st("Cs"), st("ns"), st("ms"))
```

```python
import functools

import jax
import jax.numpy as jnp
from jax import lax
from jax.experimental import pallas as pl
from jax.experimental.pallas import tpu as pltpu

F32 = jnp.float32
BF16 = jnp.bfloat16
I32 = jnp.int32

D_MODEL = 1024
PAGE_SIZE = 128
M_HEADS = 4
M_DK = 128
M_DV = 128
M_WIDTH = M_HEADS * M_DV
A_HEADS = 4
A_DH = 128
A_WIDTH = A_HEADS * A_DH
ROT_DIM = A_DH // 4
ROPE_THETA = 500000.0
IDX_HEADS = 8
IDX_DIM = 64
IDX_ROT = IDX_DIM // 4
TOPK_MAX = 256
MIX_WIDTH = M_WIDTH + A_WIDTH
LN_EPS = 1e-5

LANES = 128
NEG_BIG = -1e30
INT_MIN = -(2 ** 31)

_SPLITS = (
    ("m_q", M_WIDTH), ("m_k", M_WIDTH), ("m_v", M_WIDTH), ("m_o", M_WIDTH), ("m_z", M_WIDTH),
    ("m_i", M_HEADS), ("m_f", M_HEADS),
    ("a_q", A_WIDTH), ("a_k", A_WIDTH), ("a_v", A_WIDTH), ("a_z", A_WIDTH),
    ("i_q", IDX_HEADS * IDX_DIM), ("i_k", IDX_DIM), ("i_w", IDX_HEADS),
)
_BIG = ("m_q", "m_k", "m_v", "m_o", "m_z", "a_q", "a_k", "a_v", "a_z", "i_q")
_SM_I = IDX_DIM
_SM_F = IDX_DIM + M_HEADS
_SM_W = IDX_DIM + 2 * M_HEADS
_SM_QN = _SM_W + IDX_HEADS
_SM_KN = _SM_QN + A_HEADS
SCORE_BOUND_SQ = 1400.0
COUNT_ROWS = 128
COUNT_KEYS = 512
VMEM_LIMIT = 56 * 1024 * 1024


def _offsets():
    offs, start = {}, 0
    for name, n in _SPLITS:
        offs[name] = (start, start + n)
        start += n
    return offs


def _log_sigmoid(x):
    return jnp.minimum(x, 0.0) - jnp.log1p(jnp.exp(-jnp.abs(x)))


def _sigmoid(x):
    return 1.0 / (1.0 + jnp.exp(-x))


def _rope_tables(pos, rot_dim, period):
    half = rot_dim // 2
    inv = ROPE_THETA ** (-jnp.arange(half, dtype=F32) / half)
    ang = pos.astype(F32)[:, None] * inv
    cos, sin = jnp.cos(ang), jnp.sin(ang)
    n = pos.shape[0]
    c = jnp.concatenate([cos, cos, jnp.ones((n, period - rot_dim), F32)], axis=-1)
    s = jnp.concatenate([sin, sin, jnp.zeros((n, period - rot_dim), F32)], axis=-1)
    reps = LANES // period
    return jnp.tile(c, (1, reps)), jnp.tile(s, (1, reps))


def _rope128(x, c, s, half, period):
    lane = lax.broadcasted_iota(I32, x.shape, 1) % period
    up = pltpu.roll(x, LANES - half, axis=1)
    dn = pltpu.roll(x, half, axis=1)
    t = jnp.where(lane < half, -up, dn)
    return x * c + t * s


def _proj_kernel(x_ref, wb_ref, ws_ref, wg_ref, bb_ref, bs_ref, bg_ref,
                 ch_ref, sh_ref, ci_ref, si_ref,
                 mq_ref, mk_ref, mv_ref, mo_ref, mz_ref, aq_ref, ak_ref, av_ref, az_ref,
                 iq_ref, ik_ref, sm_ref, gt_ref, akb_ref, avb_ref, ikb_ref):
    x = x_ref[...].astype(BF16)

    def piece(j):
        w = wb_ref[:, j * 512:(j + 1) * 512]
        return jnp.dot(x, w, preferred_element_type=F32) + bb_ref[:, j * 512:(j + 1) * 512]

    def rope512(p, c, s, half, period):
        return jnp.concatenate(
            [_rope128(p[:, g * LANES:(g + 1) * LANES], c, s, half, period) for g in range(4)], axis=-1)

    ch, sh = ch_ref[...], sh_ref[...]
    ci, si = ci_ref[...], si_ref[...]
    mq_ref[...] = piece(0).astype(mq_ref.dtype)
    mk_ref[...] = (piece(1) * (M_DK ** -0.5)).astype(mk_ref.dtype)
    mv_ref[...] = piece(2).astype(mv_ref.dtype)
    mo_ref[...] = piece(3)
    mz_ref[...] = piece(4)
    aq = rope512(piece(5), ch, sh, ROT_DIM // 2, LANES)
    aq_ref[...] = (aq * (A_DH ** -0.5)).astype(BF16)
    ak = rope512(piece(6), ch, sh, ROT_DIM // 2, LANES)
    ak_ref[...] = ak
    akb_ref[...] = ak.astype(BF16)
    av = piece(7)
    av_ref[...] = av
    avb_ref[...] = av.astype(BF16)
    az_ref[...] = piece(8)
    iq_ref[...] = rope512(piece(9), ci, si, IDX_ROT // 2, IDX_DIM).astype(BF16)
    psm = jnp.dot(x, ws_ref[...], preferred_element_type=F32) + bs_ref[...]
    lane = lax.broadcasted_iota(I32, psm.shape, 1)
    sm = psm
    for base, src_ref in ((_SM_QN, aq_ref), (_SM_KN, akb_ref)):
        for h in range(A_HEADS):
            v = src_ref[:, h * A_DH:(h + 1) * A_DH].astype(F32)
            sm = jnp.where(lane == base + h, jnp.sum(v * v, axis=-1, keepdims=True), sm)
    sm_ref[...] = sm
    ik = _rope128(psm, ci, si, IDX_ROT // 2, IDX_DIM)[:, :IDX_DIM]
    ik_ref[...] = ik
    ikb_ref[...] = ik.astype(BF16)
    gt = lax.dot_general(wg_ref[...], x, (((1,), (1,)), ((), ())), preferred_element_type=F32)
    gt_ref[...] = gt + bg_ref[...]


def _const_spec(shape):
    nd = len(shape)
    return pl.BlockSpec(shape, lambda *_: (0,) * nd, pipeline_mode=pl.Buffered(1))


def _project(x2d, pos, weights, n_batch, tm, qkv_dtype):
    wb, ws, wg, bb, bs, bg = weights
    n_rows = x2d.shape[0]
    n_pos = pos.shape[0]
    npb = n_pos // tm
    ch, sh = _rope_tables(pos, ROT_DIM, LANES)
    ci, si = _rope_tables(pos, IDX_ROT, IDX_DIM)

    row = lambda p, b: (b * npb + p, 0)
    tab = lambda p, b: (p, 0)
    wide = lambda dt: jax.ShapeDtypeStruct((n_rows, 512), dt)
    out_shape = (
        wide(qkv_dtype), wide(qkv_dtype), wide(qkv_dtype), wide(F32), wide(F32),
        wide(BF16), wide(F32), wide(F32), wide(F32),
        wide(BF16),
        jax.ShapeDtypeStruct((n_rows, IDX_DIM), F32),
        jax.ShapeDtypeStruct((n_rows, LANES), F32),
        jax.ShapeDtypeStruct((8, n_rows), F32),
        wide(BF16), wide(BF16),
        jax.ShapeDtypeStruct((n_rows, IDX_DIM), BF16),
    )
    out_specs = (
        [pl.BlockSpec((tm, 512), row)] * 10
        + [pl.BlockSpec((tm, IDX_DIM), row), pl.BlockSpec((tm, LANES), row),
           pl.BlockSpec((8, tm), lambda p, b: (0, b * npb + p)),
           pl.BlockSpec((tm, 512), row), pl.BlockSpec((tm, 512), row),
           pl.BlockSpec((tm, IDX_DIM), row)]
    )
    in_specs = [
        pl.BlockSpec((tm, D_MODEL), row),
        _const_spec(wb.shape), _const_spec(ws.shape), _const_spec(wg.shape),
        _const_spec(bb.shape), _const_spec(bs.shape), _const_spec(bg.shape),
        pl.BlockSpec((tm, LANES), tab), pl.BlockSpec((tm, LANES), tab),
        pl.BlockSpec((tm, LANES), tab), pl.BlockSpec((tm, LANES), tab),
    ]
    return pl.pallas_call(
        _proj_kernel,
        grid=(npb, n_batch),
        in_specs=in_specs,
        out_specs=out_specs,
        out_shape=out_shape,
        compiler_params=pltpu.CompilerParams(
            dimension_semantics=("arbitrary", "arbitrary"), vmem_limit_bytes=VMEM_LIMIT),
        name="proj_rope",
    )(x2d, wb, ws, wg, bb, bs, bg, ch, sh, ci, si)


def _prep_weights(w_in, b_in):
    offs = _offsets()
    col = lambda nm: w_in[:, offs[nm][0]:offs[nm][1]]
    bcol = lambda nm: b_in[offs[nm][0]:offs[nm][1]]
    wb = jnp.concatenate([col(nm) for nm in _BIG], axis=1).astype(BF16)
    bb = jnp.concatenate([bcol(nm) for nm in _BIG])[None, :]
    pad = LANES - (IDX_DIM + 2 * M_HEADS + IDX_HEADS)
    ws = jnp.concatenate([col("i_k"), col("m_i"), col("m_f"), col("i_w"),
                          jnp.zeros((D_MODEL, pad), w_in.dtype)], axis=1).astype(BF16)
    bs = jnp.concatenate([bcol("i_k"), bcol("m_i"), bcol("m_f"), bcol("i_w"),
                          jnp.zeros((pad,), b_in.dtype)])[None, :]
    wg = jnp.concatenate([col("m_i"), col("m_f")], axis=1).T.astype(BF16)
    bg = jnp.concatenate([bcol("m_i"), bcol("m_f")])[:, None]
    return wb, ws, wg, bb, bs, bg


def _cumsum_rows(x):
    n = x.shape[0]
    row = lax.broadcasted_iota(I32, x.shape, 0)
    s = 1
    while s < n:
        x = x + jnp.where(row >= s, pltpu.roll(x, s, axis=0), 0.0)
        s *= 2
    return x


def _cumsum_lanes(x):
    n = x.shape[1]
    lane = lax.broadcasted_iota(I32, x.shape, 1)
    s = 1
    while s < n:
        x = x + jnp.where(lane >= s, pltpu.roll(x, s, axis=1), 0.0)
        s *= 2
    return x


def _mlstm_kernel(mq_ref, mk_ref, mv_ref, sm_ref, gt_ref, h_ref, c_ref, n_ref, m_ref):
    c_idx = pl.program_id(1)
    L = mq_ref.shape[0]

    @pl.when(c_idx == 0)
    def _():
        c_ref[...] = jnp.zeros_like(c_ref)
        n_ref[...] = jnp.zeros_like(n_ref)
        m_ref[...] = jnp.zeros_like(m_ref)

    sm = sm_ref[...]
    gt = gt_ref[...]
    bcol = _cumsum_rows(_log_sigmoid(sm))
    brow = _cumsum_lanes(_log_sigmoid(gt))
    tpos = lax.broadcasted_iota(I32, (L, L), 0)
    spos = lax.broadcasted_iota(I32, (L, L), 1)
    causal = tpos >= spos

    for h in range(M_HEADS):
        hs = slice(h * M_DK, (h + 1) * M_DK)
        q = mq_ref[:, hs]
        k = mk_ref[:, hs]
        v = mv_ref[:, hs]
        li_c = sm[:, _SM_I + h:_SM_I + h + 1]
        b_c = bcol[:, _SM_F + h:_SM_F + h + 1]
        li_r = gt[h:h + 1, :]
        b_r = brow[M_HEADS + h:M_HEADS + h + 1, :]
        m_prev = m_ref[0, h][:, :1]
        c_prev = c_ref[0, h]
        n_prev = n_ref[0, h]

        dmat = jnp.where(causal, b_c + (li_r - b_r), -jnp.inf)
        inter = b_c + m_prev
        m_row = jnp.maximum(inter, jnp.max(dmat, axis=-1, keepdims=True))
        w_intra = jnp.exp(dmat - m_row)
        w_inter = jnp.exp(inter - m_row)
        s = lax.dot_general(q, k, (((1,), (1,)), ((), ())), preferred_element_type=F32)
        qk = s * w_intra
        num = (w_inter * jnp.dot(q, c_prev.astype(BF16), preferred_element_type=F32)
               + jnp.dot(qk.astype(BF16), v, preferred_element_type=F32))
        qf = q.astype(F32)
        den = (w_inter * jnp.sum(qf * n_prev, axis=-1, keepdims=True)
               + jnp.sum(qk, axis=-1, keepdims=True))
        h_ref[:, hs] = num / jnp.maximum(jnp.abs(den), jnp.exp(-m_row))

        b_last = b_c[L - 1:L, :]
        g_r = b_last - b_r + li_r
        m_new = jnp.maximum(b_last + m_prev, jnp.max(g_r, axis=-1, keepdims=True))
        decay = jnp.exp(b_last + m_prev - m_new)
        wk_c = jnp.exp(b_last - b_c + li_c - m_new)
        kf = k.astype(F32)
        wv = (wk_c * v.astype(F32)).astype(BF16)
        c_ref[0, h] = decay * c_prev + lax.dot_general(
            k, wv, (((0,), (0,)), ((), ())), preferred_element_type=F32)
        n_ref[0, h] = decay * n_prev + jnp.sum(wk_c * kf, axis=0, keepdims=True)
        m_ref[0, h] = jnp.broadcast_to(m_new, (1, LANES))


def _mlstm_prompt(mq, mk, mv, sm, gt, n_batch, seq, chunk):
    nc = seq // chunk
    row = lambda b, c: (b * nc + c, 0)
    st = lambda b, c: (b, 0, 0, 0)
    return pl.pallas_call(
        _mlstm_kernel,
        grid=(n_batch, nc),
        in_specs=[pl.BlockSpec((chunk, 512), row)] * 3
        + [pl.BlockSpec((chunk, LANES), row), pl.BlockSpec((8, chunk), lambda b, c: (0, b * nc + c))],
        out_specs=[pl.BlockSpec((chunk, 512), row),
                   pl.BlockSpec((1, M_HEADS, M_DK, M_DV), st),
                   pl.BlockSpec((1, M_HEADS, 1, M_DK), st),
                   pl.BlockSpec((1, M_HEADS, 1, LANES), st)],
        out_shape=(jax.ShapeDtypeStruct((n_batch * seq, 512), F32),
                   jax.ShapeDtypeStruct((n_batch, M_HEADS, M_DK, M_DV), F32),
                   jax.ShapeDtypeStruct((n_batch, M_HEADS, 1, M_DK), F32),
                   jax.ShapeDtypeStruct((n_batch, M_HEADS, 1, LANES), F32)),
        compiler_params=pltpu.CompilerParams(
            dimension_semantics=("arbitrary", "arbitrary"), vmem_limit_bytes=VMEM_LIMIT),
        name="mlstm_chunkwise",
    )(mq, mk, mv, sm, gt)


def _key_to_float(key):
    return pltpu.bitcast(key ^ ((key >> 31) & 0x7FFFFFFF), F32)


def _kth_largest(count_ge, shape, topk):
    def bit_step(i, carry):
        key, n_ge = carry
        cand = key + lax.shift_left(jnp.int32(1), 31 - i)
        cnt = count_ge(_key_to_float(cand))
        ok = cnt >= topk
        return jnp.where(ok, cand, key), jnp.where(ok, cnt, n_ge)

    key, n_ge = lax.fori_loop(0, 32, bit_step,
                              (jnp.full(shape, INT_MIN, I32), jnp.full(shape, 2.0 ** 30, F32)))
    thr = _key_to_float(key)
    return jnp.where(thr > -jnp.inf, thr, -jnp.inf), n_ge


def _dsa_prompt_kernel(iq_ref, sm_ref, aq_ref, kmax_ref, kidx_ref, k_ref, v_ref, out_ref,
                       sc_ref, wslab_ref, fslab_ref, islab_ref, m_ref, l_ref, acc_ref,
                       *, tq, tk, topk, idx_bits):
    qb = pl.program_id(1)
    ntile = ((qb + 1) * tq + tk - 1) // tk
    qpos = qb * tq + lax.broadcasted_iota(I32, (tq, 1), 0)
    lane_k = lax.broadcasted_iota(I32, (tq, tk), 1)
    wsc = sm_ref[:, _SM_W:_SM_W + IDX_HEADS] * (IDX_HEADS ** -0.5) * (IDX_DIM ** -0.5)
    q_idx = [iq_ref[:, h * IDX_DIM:(h + 1) * IDX_DIM] for h in range(IDX_HEADS)]

    ncol = tk // LANES
    cols = [slice(j * LANES, (j + 1) * LANES) for j in range(ncol)]
    for h in range(IDX_HEADS):
        wslab_ref[h] = jnp.broadcast_to(wsc[:, h:h + 1], (tq, LANES))

    def score_tile(t, carry):
        kt = kidx_ref[0, pl.ds(pl.multiple_of(t * tk, tk), tk), :]
        accs = [jnp.zeros((tq, LANES), F32) for _ in cols]
        for h in range(IDX_HEADS):
            d = lax.dot_general(q_idx[h], kt, (((1,), (1,)), ((), ())), preferred_element_type=F32)
            wh = wslab_ref[h]
            accs = [acc + wh * jnp.maximum(d[:, c], 0.0) for acc, c in zip(accs, cols)]
        kpos = t * tk + lane_k
        sc_ref[t] = jnp.where(kpos <= qpos, jnp.concatenate(accs, axis=1), -jnp.inf)
        return carry

    lax.fori_loop(0, ntile, score_tile, 0)

    tpi = max(1, COUNT_KEYS // tk)
    niter = (ntile + tpi - 1) // tpi
    for extra in range(tpi - 1):
        @pl.when(ntile + extra < niter * tpi)
        def _():
            sc_ref[ntile + extra] = jnp.full((tq, tk), -jnp.inf, F32)

    rb = min(COUNT_ROWS, tq)
    lane128 = lax.broadcasted_iota(I32, (rb, LANES), 1)

    def set_f(x):
        fslab_ref[...] = jnp.broadcast_to(x, (tq, LANES))

    def set_i(x):
        islab_ref[...] = jnp.broadcast_to(x, (tq, LANES))

    def count(preds, use_pos=False):
        outs = [[] for _ in preds]
        for r0 in range(0, tq, rb):
            rows = slice(r0, r0 + rb)

            def body(p, accs, rows=rows):
                accs = list(accs)
                fv = fslab_ref[rows, :]
                iv = islab_ref[rows, :] if use_pos else None
                for t in [tpi * p + i for i in range(tpi)]:
                    for j, cs in enumerate(cols):
                        sc = sc_ref[t, rows, cs]
                        kpos = (t * tk + j * LANES) + lane128 if use_pos else None
                        for i, pred in enumerate(preds):
                            accs[i] = accs[i] + jnp.where(pred(sc, kpos, fv, iv), 1.0, 0.0)
                return tuple(accs)

            accs = lax.fori_loop(0, niter, body,
                                 tuple(jnp.zeros((rb, LANES), F32) for _ in preds))
            for out, acc in zip(outs, accs):
                out.append(acc)
        ones = jnp.ones((LANES, LANES), BF16)
        return [jnp.dot(jnp.concatenate(out, axis=0).astype(BF16), ones, preferred_element_type=F32)
                for out in outs]

    def count_ge(c):
        set_f(c)
        return count([lambda sc, kp, fv, iv: sc >= fv])[0]

    thr, n_ge = _kth_largest(count_ge, (tq, LANES), topk)
    set_f(thr)
    cut_ties = jnp.max(jnp.where(thr > -jnp.inf, n_ge - topk, 1.0)) > 0.0

    lane_q = lax.broadcasted_iota(I32, (tq, LANES), 1)

    def bias_simple():
        def body(t, carry):
            fv = fslab_ref[...]
            for cs in cols:
                sc_ref[t, :, cs] = jnp.where(sc_ref[t, :, cs] >= fv, 0.0, NEG_BIG)
            return carry
        lax.fori_loop(0, ntile, body, 0)

    def bias_cut_ties():
        n_gt, = count([lambda sc, kp, fv, iv: sc > fv])
        need = topk - n_gt

        def idx_step(i, jcut):
            cand = jcut + lax.shift_left(jnp.int32(1), idx_bits - 1 - i)
            set_i(cand)
            cnt, = count([lambda sc, kp, fv, iv: (sc == fv) & (kp < iv)], use_pos=True)
            return jnp.where(cnt < need, cand, jcut)

        jcut = lax.fori_loop(0, idx_bits, idx_step, jnp.zeros((tq, LANES), I32))
        set_i(jnp.where(thr > -jnp.inf, jcut, -1))

        def body(t, carry):
            fv = fslab_ref[...]
            iv = islab_ref[...]
            for j, cs in enumerate(cols):
                sc = sc_ref[t, :, cs]
                sel = (sc > fv) | ((sc == fv) & ((t * tk + j * LANES) + lane_q <= iv))
                sc_ref[t, :, cs] = jnp.where(sel, 0.0, NEG_BIG)
            return carry
        lax.fori_loop(0, ntile, body, 0)

    lax.cond(cut_ties, bias_cut_ties, bias_simple)

    def scores(t, h):
        hs = slice(h * A_DH, (h + 1) * A_DH)
        kh = k_ref[0, pl.ds(pl.multiple_of(t * tk, tk), tk), hs]
        s = lax.dot_general(aq_ref[:, hs], kh, (((1,), (1,)), ((), ())), preferred_element_type=F32)
        return s + sc_ref[t]

    l_ref[...] = jnp.zeros_like(l_ref)
    acc_ref[...] = jnp.zeros_like(acc_ref)

    def row_max():
        m_ref[...] = jnp.full_like(m_ref, NEG_BIG)

        def max_sweep(t, carry):
            for h in range(A_HEADS):
                s = scores(t, h)
                m = m_ref[h]
                for c in cols:
                    m = jnp.maximum(m, s[:, c])
                m_ref[h] = m
            return carry

        lax.fori_loop(0, ntile, max_sweep, 0)
        for h in range(A_HEADS):
            m_ref[h] = jnp.broadcast_to(jnp.max(m_ref[h], axis=-1, keepdims=True), (tq, LANES))

    def no_max():
        m_ref[...] = jnp.zeros_like(m_ref)

    bound_sq = sm_ref[:, _SM_QN:_SM_QN + A_HEADS] * kmax_ref[0]
    lax.cond(jnp.max(bound_sq) <= SCORE_BOUND_SQ, no_max, row_max)

    def sum_sweep(t, carry):
        start = pl.multiple_of(t * tk, tk)
        for h in range(A_HEADS):
            s = scores(t, h)
            mb = m_ref[h]
            ps = [jnp.exp(s[:, c] - mb) for c in cols]
            lsum = ps[0]
            for pj in ps[1:]:
                lsum = lsum + pj
            l_ref[h] = l_ref[h] + lsum
            vh = v_ref[0, pl.ds(start, tk), h * A_DH:(h + 1) * A_DH]
            p = jnp.concatenate(ps, axis=1).astype(BF16)
            acc_ref[h] = acc_ref[h] + jnp.dot(p, vh, preferred_element_type=F32)
        return carry

    lax.fori_loop(0, ntile, sum_sweep, 0)
    for h in range(A_HEADS):
        out_ref[:, h * A_DH:(h + 1) * A_DH] = acc_ref[h] / jnp.sum(l_ref[h], axis=-1, keepdims=True)


def _dsa_prompt(iq, sm, aq, ikb, akb, avb, n_batch, seq, tq, tk):
    nq = seq // tq
    topk = min(TOPK_MAX, seq // 4)
    row = lambda b, q: (b * nq + q, 0)
    full = lambda b, q: (b, 0, 0)
    kern = functools.partial(_dsa_prompt_kernel, tq=tq, tk=tk, topk=topk,
                             idx_bits=(seq - 1).bit_length())
    kmax = jnp.max(sm[:, _SM_KN:_SM_KN + A_HEADS].reshape(n_batch, seq, A_HEADS), axis=1, keepdims=True)
    return pl.pallas_call(
        kern,
        grid=(n_batch, nq),
        in_specs=[pl.BlockSpec((tq, 512), row), pl.BlockSpec((tq, LANES), row),
                  pl.BlockSpec((tq, 512), row), pl.BlockSpec((1, 1, A_HEADS), full),
                  pl.BlockSpec((1, seq, IDX_DIM), full, pipeline_mode=pl.Buffered(1)),
                  pl.BlockSpec((1, seq, 512), full, pipeline_mode=pl.Buffered(1)),
                  pl.BlockSpec((1, seq, 512), full, pipeline_mode=pl.Buffered(1))],
        out_specs=pl.BlockSpec((tq, 512), row),
        out_shape=jax.ShapeDtypeStruct((n_batch * seq, 512), F32),
        scratch_shapes=[pltpu.VMEM((seq // tk + max(1, COUNT_KEYS // tk) - 1, tq, tk), F32),
                        pltpu.VMEM((IDX_HEADS, tq, LANES), F32),
                        pltpu.VMEM((tq, LANES), F32), pltpu.VMEM((tq, LANES), I32),
                        pltpu.VMEM((A_HEADS, tq, LANES), F32), pltpu.VMEM((A_HEADS, tq, LANES), F32),
                        pltpu.VMEM((A_HEADS, tq, A_DH), F32)],
        compiler_params=pltpu.CompilerParams(
            dimension_semantics=("arbitrary", "arbitrary"), vmem_limit_bytes=VMEM_LIMIT),
        name="dsa_prompt",
    )(iq, sm, aq, kmax, ikb.reshape(n_batch, seq, IDX_DIM), akb.reshape(n_batch, seq, 512),
      avb.reshape(n_batch, seq, 512))


def _out_kernel(x_ref, hm_ref, mo_ref, mz_ref, ha_ref, az_ref, gain_ref, wo_ref, g_ref, b_ref,
                y_ref, *, alpha):
    hm = hm_ref[...] * _sigmoid(mo_ref[...])
    parts = []
    for h in range(M_HEADS):
        hh = hm[:, h * M_DV:(h + 1) * M_DV]
        mu = jnp.mean(hh, axis=-1, keepdims=True)
        var = jnp.mean(jnp.square(hh - mu), axis=-1, keepdims=True)
        parts.append((hh - mu) * lax.rsqrt(var + LN_EPS))
    mz = mz_ref[...]
    az = az_ref[...]
    g_m = jnp.concatenate(parts, axis=-1) * gain_ref[...] * (mz * _sigmoid(mz))
    g_a = ha_ref[...] * (az * _sigmoid(az))
    mixed = jnp.concatenate([g_m, g_a], axis=-1).astype(BF16)
    sub = jnp.dot(mixed, wo_ref[...], preferred_element_type=F32)
    r = alpha * x_ref[...] + sub
    mu = jnp.mean(r, axis=-1, keepdims=True)
    var = jnp.mean(jnp.square(r - mu), axis=-1, keepdims=True)
    y_ref[...] = (r - mu) * lax.rsqrt(var + LN_EPS) * g_ref[...] + b_ref[...]


def _out_mix(x2d, hm, mo, mz, ha, az, gain, wo, ln_g, ln_b, alpha, tm):
    n_rows = x2d.shape[0]
    row = lambda i: (i, 0)
    return pl.pallas_call(
        functools.partial(_out_kernel, alpha=alpha),
        grid=(n_rows // tm,),
        in_specs=[pl.BlockSpec((tm, D_MODEL), row)] + [pl.BlockSpec((tm, 512), row)] * 5
        + [_const_spec((1, M_WIDTH)), _const_spec((MIX_WIDTH, D_MODEL)),
           _const_spec((1, D_MODEL)), _const_spec((1, D_MODEL))],
        out_specs=pl.BlockSpec((tm, D_MODEL), row),
        out_shape=jax.ShapeDtypeStruct((n_rows, D_MODEL), F32),
        compiler_params=pltpu.CompilerParams(
            dimension_semantics=("arbitrary",), vmem_limit_bytes=VMEM_LIMIT),
        name="out_mix",
    )(x2d, hm, mo, mz, ha, az, gain[None, :], wo.astype(BF16), ln_g[None, :], ln_b[None, :])


def _mlstm_step_kernel(qc_ref, kc_ref, qr_ref, kr_ref, vr_ref, i_ref, f_ref, c0_ref, n0_ref, m0_ref,
                       h_ref, c_ref, n_ref, m_ref):
    lf = _log_sigmoid(f_ref[...])
    m0 = m0_ref[...]
    it = i_ref[...]
    m_new = jnp.maximum(lf + m0, it)
    fg = jnp.exp(lf + m0 - m_new)
    ig = jnp.exp(it - m_new)
    c_new = fg * c0_ref[...] + (ig * kc_ref[...]) * vr_ref[...]
    n_new = fg * n0_ref[...] + ig * kr_ref[...]
    num = jnp.sum(qc_ref[...] * c_new, axis=2, keepdims=True)
    den = jnp.sum(qr_ref[...] * n_new, axis=3, keepdims=True)
    h_ref[...] = num / jnp.maximum(jnp.abs(den), jnp.exp(-m_new))
    c_ref[...] = c_new
    n_ref[...] = n_new
    m_ref[...] = m_new


def _mlstm_step(mq, mk, mv, sm, c0, n0, m0, tb):
    db = mq.shape[0]
    col = lambda t: t.reshape(db, M_HEADS, M_DK, 1)
    rowv = lambda t: t.reshape(db, M_HEADS, 1, M_DK)
    sc = lambda t: t.reshape(db, M_HEADS, 1, 1)
    idx = lambda i: (i, 0, 0, 0)
    s_col = pl.BlockSpec((tb, M_HEADS, M_DK, 1), idx)
    s_row = pl.BlockSpec((tb, M_HEADS, 1, M_DK), idx)
    s_sc = pl.BlockSpec((tb, M_HEADS, 1, 1), idx)
    s_mat = pl.BlockSpec((tb, M_HEADS, M_DK, M_DV), idx)
    return pl.pallas_call(
        _mlstm_step_kernel,
        grid=(db // tb,),
        in_specs=[s_col, s_col, s_row, s_row, s_row, s_sc, s_sc, s_mat, s_row, s_sc],
        out_specs=[s_row, s_mat, s_row, s_sc],
        out_shape=(jax.ShapeDtypeStruct((db, M_HEADS, 1, M_DV), F32),
                   jax.ShapeDtypeStruct((db, M_HEADS, M_DK, M_DV), F32),
                   jax.ShapeDtypeStruct((db, M_HEADS, 1, M_DK), F32),
                   jax.ShapeDtypeStruct((db, M_HEADS, 1, 1), F32)),
        compiler_params=pltpu.CompilerParams(
            dimension_semantics=("arbitrary",), vmem_limit_bytes=VMEM_LIMIT),
        name="mlstm_step",
    )(col(mq), col(mk), rowv(mq), rowv(mk), rowv(mv),
      sc(sm[:, _SM_I:_SM_I + M_HEADS]), sc(sm[:, _SM_F:_SM_F + M_HEADS]),
      c0, rowv(n0), sc(m0))


SCORE_ROWS_PAD = 8


def _samp_index_kernel(pt_ref, q_ref, w_ref, knew_ref, kidx_hbm, out_ref, buf, sem, *, n_pages):
    b = pl.program_id(0)
    nb = pl.num_programs(0)
    slot = b % 2

    def copies(seq, sl):
        return [pltpu.make_async_copy(kidx_hbm.at[pt_ref[seq, p]], buf.at[sl, p], sem.at[sl])
                for p in range(n_pages)]

    @pl.when(b == 0)
    def _():
        for cp in copies(0, 0):
            cp.start()

    @pl.when(b + 1 < nb)
    def _():
        for cp in copies(b + 1, 1 - slot):
            cp.start()

    for cp in copies(b, slot):
        cp.wait()

    q = q_ref[0]
    wcol = w_ref[0] * (IDX_HEADS ** -0.5) * (IDX_DIM ** -0.5)

    gp = 16 if n_pages % 16 == 0 else 8

    def group(g, carry):
        rows = []
        for j in range(gp):
            kt = buf[slot, g * gp + j].astype(BF16)
            d = jnp.dot(q, kt, preferred_element_type=F32)
            rows.append(jnp.sum(jnp.maximum(d, 0.0) * wcol, axis=0, keepdims=True))
        out_ref[0, pl.ds(pl.multiple_of(g * gp, gp), gp), :] = jnp.concatenate(rows, axis=0)
        return carry

    lax.fori_loop(0, n_pages // gp, group, 0)
    kn = knew_ref[0].astype(BF16).astype(F32)
    dn = jnp.sum(q.astype(F32) * kn, axis=-1, keepdims=True)
    scn = jnp.sum(jnp.maximum(dn, 0.0) * wcol, axis=0, keepdims=True)
    lane = lax.broadcasted_iota(I32, (SCORE_ROWS_PAD, LANES), 1)
    row = lax.broadcasted_iota(I32, (SCORE_ROWS_PAD, LANES), 0)
    out_ref[0, n_pages:n_pages + SCORE_ROWS_PAD, :] = jnp.where((lane == 0) & (row == 0), scn, -jnp.inf)


def _samp_index(page_table, q, wcol, knew, kidx_t):
    db, n_pages = page_table.shape
    assert n_pages % 8 == 0
    rows = n_pages + SCORE_ROWS_PAD
    per_seq = lambda shape: pl.BlockSpec((1,) + shape, lambda b, pt: (b, 0, 0))
    grid_spec = pltpu.PrefetchScalarGridSpec(
        num_scalar_prefetch=1,
        grid=(db,),
        in_specs=[per_seq((IDX_HEADS, IDX_DIM)), per_seq((IDX_HEADS, 1)), per_seq((1, IDX_DIM)),
                  pl.BlockSpec(memory_space=pl.ANY)],
        out_specs=per_seq((rows, LANES)),
        scratch_shapes=[pltpu.VMEM((2, n_pages, IDX_DIM, PAGE_SIZE), F32),
                        pltpu.SemaphoreType.DMA((2,))],
    )
    return pl.pallas_call(
        functools.partial(_samp_index_kernel, n_pages=n_pages),
        grid_spec=grid_spec,
        out_shape=jax.ShapeDtypeStruct((db, rows, LANES), F32),
        compiler_params=pltpu.CompilerParams(
            dimension_semantics=("arbitrary",), vmem_limit_bytes=VMEM_LIMIT),
        name="samp_index",
    )(page_table, q, wcol, knew, kidx_t)


def _samp_topk_kernel(sc_ref, pt_ref, rid_ref, meta_ref, *, topk, idx_bits, n_pages):
    sc = sc_ref[...]
    tb, rows, _ = sc.shape
    kpos = (lax.broadcasted_iota(I32, sc.shape, 1) * PAGE_SIZE
            + lax.broadcasted_iota(I32, sc.shape, 2))

    def count(m):
        c = jnp.sum(jnp.where(m, 1.0, 0.0), axis=1, keepdims=True)
        return jnp.sum(c, axis=2, keepdims=True)

    thr, _ = _kth_largest(lambda c: count(sc >= c), (tb, 1, 1), topk)
    need = topk - count(sc > thr)

    def idx_step(i, jcut):
        cand = jcut + lax.shift_left(jnp.int32(1), idx_bits - 1 - i)
        return jnp.where(count((sc == thr) & (kpos < cand)) < need, cand, jcut)

    jcut = lax.fori_loop(0, idx_bits, idx_step, jnp.zeros((tb, 1, 1), I32))
    jcut = jnp.where(thr > -jnp.inf, jcut, -1)
    sel = jnp.where((sc > thr) | ((sc == thr) & (kpos <= jcut)), 1.0, 0.0)

    slot = lax.broadcasted_iota(I32, (topk, LANES), 0).astype(F32)
    lane = lax.broadcasted_iota(I32, (topk, LANES), 1)
    lane1 = lax.broadcasted_iota(I32, (1, LANES), 1)
    ones8 = jnp.ones((8, LANES), BF16)
    zpad = jnp.zeros((LANES - rows, LANES), F32)
    for i in range(tb):
        sel_i = jnp.concatenate([sel[i], zpad], axis=0)
        incl = _cumsum_lanes(sel_i)
        tot = lax.dot_general(ones8, sel_i.astype(BF16), (((1,), (1,)), ((), ())),
                              preferred_element_type=F32)[0:1, :]
        end = _cumsum_lanes(tot)
        start = end - tot
        rj = jnp.sum(jnp.where(end <= slot, 1.0, 0.0), axis=-1, keepdims=True)
        onehot = jnp.where(lane.astype(F32) == rj, 1.0, 0.0)
        local = slot[:, :1] - jnp.sum(onehot * start, axis=-1, keepdims=True)
        incl_j = jnp.dot(onehot.astype(BF16), incl.astype(BF16), preferred_element_type=F32)
        off = jnp.sum(jnp.where(incl_j <= local, 1.0, 0.0), axis=-1, keepdims=True)
        phys = jnp.sum(onehot * pt_ref[i], axis=-1, keepdims=True)
        rid_ref[i] = (phys * PAGE_SIZE + off).astype(I32)
        n_past = end[:, n_pages - 1:n_pages]
        new_sel = tot[:, n_pages:n_pages + 1]
        meta_ref[i] = jnp.where(lane1 == 0, n_past, jnp.where(lane1 == 1, new_sel, 0.0)).astype(I32)


def _samp_topk(sc, ptf, n_keys, n_pages, tb):
    db, rows, _ = sc.shape
    topk = min(TOPK_MAX, n_keys // 4)
    blk = lambda shape: pl.BlockSpec((tb,) + shape, lambda i: (i, 0, 0))
    return pl.pallas_call(
        functools.partial(_samp_topk_kernel, topk=topk, idx_bits=(n_keys - 1).bit_length(),
                          n_pages=n_pages),
        grid=(db // tb,),
        in_specs=[blk((rows, LANES)), blk((1, LANES))],
        out_specs=[blk((topk, 1)), blk((1, LANES))],
        out_shape=(jax.ShapeDtypeStruct((db, topk, 1), I32), jax.ShapeDtypeStruct((db, 1, LANES), I32)),
        compiler_params=pltpu.CompilerParams(
            dimension_semantics=("arbitrary",), vmem_limit_bytes=VMEM_LIMIT),
        name="samp_topk",
    )(sc, ptf)


PAGE_SHIFT = PAGE_SIZE.bit_length() - 1


def _samp_attn_kernel(rid_ref, meta_ref, q_ref, knew_ref, vnew_ref, k_hbm, v_hbm, out_ref,
                      kbuf, vbuf, ksem, vsem, *, layer, nsel):
    b = pl.program_id(0)
    nb = pl.num_programs(0)
    slot = b % 2

    def row_copies(page, off, sl, j):
        return (pltpu.make_async_copy(k_hbm.at[layer, page, off], kbuf.at[sl, j], ksem.at[sl]),
                pltpu.make_async_copy(v_hbm.at[layer, page, off], vbuf.at[sl, j], vsem.at[sl]))

    def issue(seq, sl):
        def body(g, carry):
            for u in range(8):
                j = g * 8 + u
                rid = rid_ref[seq, j]
                page = lax.shift_right_logical(rid, jnp.int32(PAGE_SHIFT))
                for cp in row_copies(page, rid & (PAGE_SIZE - 1), sl, j):
                    cp.start(priority=u % 2)
            return carry
        lax.fori_loop(0, nsel // 8, body, 0)

    @pl.when(b == 0)
    def _():
        issue(0, 0)

    @pl.when(b + 1 < nb)
    def _():
        issue(b + 1, 1 - slot)

    pltpu.make_async_copy(kbuf.at[slot], kbuf.at[slot], ksem.at[slot]).wait()
    pltpu.make_async_copy(vbuf.at[slot], vbuf.at[slot], vsem.at[slot]).wait()

    n_past = meta_ref[b, 0]
    has_new = meta_ref[b, 1] > 0
    q = q_ref[...]
    s = jnp.sum(kbuf[slot] * q, axis=-1, keepdims=True)
    valid = lax.broadcasted_iota(I32, s.shape, 0) < n_past
    s = jnp.where(valid, s, NEG_BIG)
    s_new = jnp.sum(knew_ref[...] * q, axis=-1, keepdims=True)
    m = jnp.max(s, axis=0, keepdims=True)
    m = jnp.where(has_new, jnp.maximum(m, s_new), m)
    p = jnp.where(valid, jnp.exp(s - m), 0.0)
    p_new = jnp.where(has_new, jnp.exp(s_new - m), 0.0)
    l = jnp.sum(p, axis=0, keepdims=True) + p_new
    o = jnp.sum(p * vbuf[slot], axis=0, keepdims=True) + p_new * vnew_ref[...]
    out_ref[...] = o / l


def _samp_attn(rid, meta, q, knew, vnew, k_cache, v_cache, layer):
    db, nsel = rid.shape
    per_seq = pl.BlockSpec((1, A_HEADS, A_DH), lambda b, r, m: (b, 0, 0))
    grid_spec = pltpu.PrefetchScalarGridSpec(
        num_scalar_prefetch=2,
        grid=(db,),
        in_specs=[per_seq, per_seq, per_seq,
                  pl.BlockSpec(memory_space=pl.ANY), pl.BlockSpec(memory_space=pl.ANY)],
        out_specs=per_seq,
        scratch_shapes=[pltpu.VMEM((2, nsel, A_HEADS, A_DH), F32),
                        pltpu.VMEM((2, nsel, A_HEADS, A_DH), F32),
                        pltpu.SemaphoreType.DMA((2,)), pltpu.SemaphoreType.DMA((2,))],
    )
    return pl.pallas_call(
        functools.partial(_samp_attn_kernel, layer=layer, nsel=nsel),
        grid_spec=grid_spec,
        out_shape=jax.ShapeDtypeStruct((db, A_HEADS, A_DH), F32),
        compiler_params=pltpu.CompilerParams(
            dimension_semantics=("arbitrary",), vmem_limit_bytes=VMEM_LIMIT),
        name="samp_attn",
    )(rid, meta, q, knew, vnew, k_cache, v_cache)


def kernel(x_prompt, x_sample, state_C, state_n, state_m, cache_k, cache_v, cache_kidx, page_table,
           w_in, b_in, mh_gain, w_out, ln_g, ln_b):
    depth = w_in.shape[0]
    n_batch, seq, _ = x_prompt.shape
    db, dec_seq, _ = x_sample.shape
    assert dec_seq == 1, "one new token per sampled sequence"
    n_pages = page_table.shape[1]
    past = n_pages * PAGE_SIZE
    alpha = (2.0 * depth) ** 0.25
    pos_p = jnp.arange(seq)
    pos_s = past + jnp.arange(dec_seq)
    tm = min(512, seq)
    chunk = min(512, seq)
    tq, tk = min(512, seq), min(512, seq)

    hp = x_prompt.reshape(n_batch * seq, D_MODEL)
    hs = x_sample.reshape(db * dec_seq, D_MODEL)
    outs = {k: [] for k in ("kp", "vp", "ip", "Cp", "np", "mp", "ks", "vs", "is", "Cs", "ns", "ms")}
    for l in range(depth):
        weights = _prep_weights(w_in[l], b_in[l])
        (mq, mk, mv, mo, mz, aq, ak, av, az, iq, ik, sm, gt, akb, avb, ikb) = _project(
            hp, pos_p, weights, n_batch, tm, BF16)
        h_m, c_p, n_p, m_p = _mlstm_prompt(mq, mk, mv, sm, gt, n_batch, seq, chunk)
        h_a = _dsa_prompt(iq, sm, aq, ikb, akb, avb, n_batch, seq, tq, tk)
        hp_new = _out_mix(hp, h_m, mo, mz, h_a, az, mh_gain[l], w_out[l], ln_g[l], ln_b[l], alpha, tm)
        outs["kp"].append(ak.reshape(n_batch, seq, A_HEADS, A_DH))
        outs["vp"].append(av.reshape(n_batch, seq, A_HEADS, A_DH))
        outs["ip"].append(ik.reshape(n_batch, seq, IDX_DIM))
        outs["Cp"].append(c_p)
        outs["np"].append(n_p.reshape(n_batch, M_HEADS, M_DK))
        outs["mp"].append(m_p[:, :, 0, 0])
        hp = hp_new
        (smq, smk, smv, smo, smz, saq, sak, sav, saz, siq, sik, ssm, _, _, _, _) = _project(
            hs, jnp.tile(pos_s, db), weights, 1, min(128, db * dec_seq), F32)
        g_m, c_s, n_s, m_s = _mlstm_step(smq, smk, smv, ssm, state_C[l], state_n[l], state_m[l],
                                         tb=min(8, db))
        kidx_t = jnp.swapaxes(cache_kidx[l], 1, 2)
        sc = _samp_index(page_table, siq.reshape(db, IDX_HEADS, IDX_DIM),
                         ssm[:, _SM_W:_SM_W + IDX_HEADS, None], sik[:, None, :], kidx_t)
        ptf = jnp.pad(page_table.astype(F32), ((0, 0), (0, LANES - n_pages)))[:, None, :]
        rid, meta = _samp_topk(sc, ptf, past + dec_seq, n_pages, tb=min(8, db))
        heads = lambda t: t.astype(F32).reshape(db, A_HEADS, A_DH)
        g_a = _samp_attn(rid[:, :, 0], meta[:, 0, :2], heads(saq), heads(sak), heads(sav),
                         cache_k, cache_v, l)
        hs_new = _out_mix(hs, g_m.reshape(db, M_WIDTH), smo, smz, g_a.reshape(db, A_WIDTH), saz,
                          mh_gain[l], w_out[l], ln_g[l], ln_b[l], alpha, min(128, db))
        outs["ks"].append(sak.reshape(db, dec_seq, A_HEADS, A_DH))
        outs["vs"].append(sav.reshape(db, dec_seq, A_HEADS, A_DH))
        outs["is"].append(sik.reshape(db, dec_seq, IDX_DIM))
        outs["Cs"].append(c_s)
        outs["ns"].append(n_s.reshape(db, M_HEADS, M_DK))
        outs["ms"].append(m_s.reshape(db, M_HEADS))
        hs = hs_new
    st = lambda k: jnp.stack(outs[k])
    return (hp.reshape(n_batch, seq, D_MODEL), hs.reshape(db, dec_seq, D_MODEL),
            st("kp"), st("vp"), st("ip"), st("Cp"), st("np"), st("mp"),
            st("ks"), st("vs"), st("is"), st("Cs"), st("ns"), st("ms"))
```
